```python
import jax, jax.numpy as jnp
from jax import lax
import numpy as np

D_MODEL = 1024
BATCH = 8
SEQ = 2048
DEPTH = 4
DEC_BATCH = 32
DEC_SEQ = 4
PAST_LEN = 8192
PAGE_SIZE = 128

N_MIXERS = 2
N_CHUNK_LAYERS = (DEPTH + 1) // 2
N_ATTN_LAYERS = DEPTH // 2
CHUNK = 128
CHUNK_WIDTH = D_MODEL
CHUNK_GROUPS = 4
CHUNK_GROUP_DIM = CHUNK_WIDTH // CHUNK_GROUPS
HEAD_DIM = 128
N_HEADS = D_MODEL // HEAD_DIM
N_KV_HEADS = N_HEADS // 2
KV_GROUP = N_HEADS // N_KV_HEADS
ROT_DIM = HEAD_DIM // 4
ROPE_THETA = 500000.0
MOBA_BLOCK = 256
MOBA_TOP_K = 3
Q_BLOCK = 8
N_EXPERT_GROUPS = 4
EXPERTS_PER_GROUP = 4
N_EXPERTS = N_EXPERT_GROUPS * EXPERTS_PER_GROUP
TOP_K_IN_GROUP = 2
D_EXPERT = D_MODEL // 4
NORM_EPS = 1e-6
NEG_INF = -1e30

kernel_name = 'hybrid_chunkmlp_moba_hmoe_step'


def rms_norm(x, g):
    xf = x.astype(jnp.float32)
    y = xf * lax.rsqrt(jnp.mean(xf * xf, axis=-1, keepdims=True) + NORM_EPS)
    return (y * g.astype(jnp.float32)).astype(x.dtype)


def rope(x, pos):
    half = ROT_DIM // 2
    inv_freq = jnp.power(ROPE_THETA, -jnp.arange(half, dtype=jnp.float32) * 2.0 / ROT_DIM)
    ang = pos.astype(jnp.float32)[:, None] * inv_freq[None, :]
    cos = jnp.cos(ang)[None, :, None, :]
    sin = jnp.sin(ang)[None, :, None, :]
    xr = x[..., :ROT_DIM].astype(jnp.float32)
    x1, x2 = xr[..., :half], xr[..., half:]
    rot = jnp.concatenate([x1 * cos - x2 * sin, x2 * cos + x1 * sin], axis=-1).astype(x.dtype)
    return jnp.concatenate([rot, x[..., ROT_DIM:]], axis=-1)


def chunk_spatial_gating(h, w_in, g_v, w_s, b_s, w_out):
    B, L, _ = h.shape
    z = jax.nn.gelu(h @ w_in)
    u, v = z[..., :CHUNK_WIDTH], z[..., CHUNK_WIDTH:]
    v = rms_norm(v, g_v)
    n_chunks = -(-L // CHUNK)
    pad = n_chunks * CHUNK - L
    vc = jnp.pad(v, ((0, 0), (0, pad), (0, 0))).reshape(B, n_chunks, CHUNK, CHUNK_GROUPS, CHUNK_GROUP_DIM)
    causal = jnp.tril(jnp.ones((CHUNK, CHUNK), w_s.dtype))
    s = jnp.einsum('gts,bcsgd->bctgd', w_s * causal, vc) + jnp.transpose(b_s)[None, None, :, :, None]
    s = s.reshape(B, n_chunks * CHUNK, CHUNK_WIDTH)[:, :L]
    return (u * s) @ w_out, v


def attn_qkv(h, pos, w_qkv, q_norm, k_norm):
    B, T, _ = h.shape
    qkv = h @ w_qkv
    nq = N_HEADS * HEAD_DIM
    nk = N_KV_HEADS * HEAD_DIM
    q = qkv[..., :nq].reshape(B, T, N_HEADS, HEAD_DIM)
    k = qkv[..., nq:nq + nk].reshape(B, T, N_KV_HEADS, HEAD_DIM)
    v = qkv[..., nq + nk:].reshape(B, T, N_KV_HEADS, HEAD_DIM)
    q = rope(rms_norm(q, q_norm), pos)
    k = rope(rms_norm(k, k_norm), pos)
    return q, k, v


def moba_attention(q, k, v, q_pos):
    B, T = q.shape[0], q.shape[1]
    L = k.shape[1]
    nb = -(-L // MOBA_BLOCK)
    pad = nb * MOBA_BLOCK - L
    kb = jnp.pad(k, ((0, 0), (0, pad), (0, 0), (0, 0))).reshape(B, nb, MOBA_BLOCK, N_KV_HEADS, HEAD_DIM)
    vb = jnp.pad(v, ((0, 0), (0, pad), (0, 0), (0, 0))).reshape(B, nb, MOBA_BLOCK, N_KV_HEADS, HEAD_DIM)
    k_mean = jnp.mean(kb.astype(jnp.float32), axis=2)
    n_sel = min(MOBA_TOP_K, nb)
    qb = min(Q_BLOCK, T)
    n_qb = -(-T // qb)
    padq = n_qb * qb - T
    q_blocks = jnp.pad(q, ((0, 0), (0, padq), (0, 0), (0, 0))).reshape(B, n_qb, qb, N_HEADS, HEAD_DIM).transpose(1, 0, 2, 3, 4)
    pos_blocks = jnp.pad(q_pos, (0, padq), mode='edge').reshape(n_qb, qb)
    b_idx = jnp.arange(B)[:, None, None, None]
    h_idx = (jnp.arange(N_HEADS) // KV_GROUP)[None, None, :, None]
    blk_ids = jnp.arange(nb)
    sel_rank = jnp.arange(n_sel)
    in_blk = jnp.arange(MOBA_BLOCK)
    scale = HEAD_DIM ** -0.5

    def attend_block(args):
        qc, pc = args
        own = pc // MOBA_BLOCK
        qg = qc.astype(jnp.float32).reshape(B, qb, N_KV_HEADS, KV_GROUP, HEAD_DIM)
        gate = jnp.einsum('bqkgd,bnkd->bqkgn', qg, k_mean).reshape(B, qb, N_HEADS, nb)
        fully_past = blk_ids[None, :] < own[:, None]
        gate = jnp.where(fully_past[None, :, None, :], gate, NEG_INF)
        _, sel = lax.top_k(gate, n_sel)
        own_b = jnp.broadcast_to(own[None, :, None, None], (B, qb, N_HEADS, 1)).astype(sel.dtype)
        idx = jnp.concatenate([sel, own_b], axis=-1)
        gk = kb[b_idx, idx, :, h_idx]
        gv = vb[b_idx, idx, :, h_idx]
        blk_ok = jnp.concatenate([sel_rank[None, :] < own[:, None], jnp.ones((qb, 1), bool)], axis=-1)
        key_pos = idx[..., None] * MOBA_BLOCK + in_blk
        mask = blk_ok[None, :, None, :, None] & (key_pos <= pc[None, :, None, None, None])
        s = jnp.einsum('bqhd,bqhjkd->bqhjk', qc * scale, gk).astype(jnp.float32)
        s = jnp.where(mask, s, NEG_INF)
        p = jax.nn.softmax(s.reshape(B, qb, N_HEADS, -1), axis=-1).reshape(s.shape)
        return jnp.einsum('bqhjk,bqhjkd->bqhd', p.astype(gv.dtype), gv)

    out = lax.map(attend_block, (q_blocks, pos_blocks))
    return out.transpose(1, 0, 2, 3, 4).reshape(B, n_qb * qb, N_HEADS, HEAD_DIM)[:, :T]


def hier_moe(h, w_group, b_group, w_router, b_router, w_gate, w_up, w_down):
    B, T, D = h.shape
    x = h.reshape(B * T, D)
    g_prob = jax.nn.softmax((x @ w_group).astype(jnp.float32) + b_group.astype(jnp.float32), axis=-1)
    g_w, g_idx = lax.top_k(g_prob, 1)
    e_logits = ((x @ w_router).astype(jnp.float32) + b_router.astype(jnp.float32)).reshape(-1, N_EXPERT_GROUPS, EXPERTS_PER_GROUP)
    take = jnp.broadcast_to(g_idx[:, :, None], (g_idx.shape[0], 1, EXPERTS_PER_GROUP))
    e_logits = jnp.take_along_axis(e_logits, take, axis=1)[:, 0]
    e_prob = jax.nn.softmax(e_logits, axis=-1)
    e_w, e_idx = lax.top_k(e_prob, TOP_K_IN_GROUP)
    e_w = e_w / jnp.sum(e_w, axis=-1, keepdims=True) * g_w
    expert = g_idx * EXPERTS_PER_GROUP + e_idx
    gates = jnp.sum(jax.nn.one_hot(expert, N_EXPERTS, dtype=jnp.float32) * e_w[..., None], axis=1)
    a = jnp.einsum('nd,edf->nef', x, w_gate)
    b = jnp.einsum('nd,edf->nef', x, w_up)
    hid = jax.nn.silu(a) * b * gates[:, :, None].astype(x.dtype)
    y = jnp.einsum('nef,efd->nd', hid, w_down)
    return y.reshape(B, T, D)


def setup_inputs(seed: int = 0) -> dict:
    key = jax.random.key(seed)
    ks = jax.random.split(key, 24)
    f32 = jnp.float32
    n_pages = PAST_LEN // PAGE_SIZE
    n_pool = (DEC_BATCH * n_pages * 5) // 4
    qkv_cols = (N_HEADS + 2 * N_KV_HEADS) * HEAD_DIM

    def nrm(k, shape, scale=1.0):
        return jax.random.normal(k, shape, f32) * scale

    def gain(k, shape):
        return 1.0 + 0.02 * jax.random.normal(k, shape, f32)

    page_table = jax.random.permutation(ks[4], n_pool)[: DEC_BATCH * n_pages].reshape(DEC_BATCH, n_pages).astype(jnp.int32)
    return {
        'x_prompt': nrm(ks[0], (BATCH, SEQ, D_MODEL)),
        'x_sample': nrm(ks[1], (DEC_BATCH, DEC_SEQ, D_MODEL)),
        'cache_k': nrm(ks[2], (N_ATTN_LAYERS, n_pool, PAGE_SIZE, N_KV_HEADS, HEAD_DIM)),
        'cache_v': nrm(ks[3], (N_ATTN_LAYERS, n_pool, PAGE_SIZE, N_KV_HEADS, HEAD_DIM)),
        'page_table': page_table,
        'norm1': gain(ks[5], (DEPTH, D_MODEL)),
        'norm2': gain(ks[6], (DEPTH, D_MODEL)),
        'a_w_in': nrm(ks[7], (N_CHUNK_LAYERS, D_MODEL, 2 * CHUNK_WIDTH), D_MODEL ** -0.5),
        'a_g_v': gain(ks[8], (N_CHUNK_LAYERS, CHUNK_WIDTH)),
        'a_w_s': nrm(ks[9], (N_CHUNK_LAYERS, CHUNK_GROUPS, CHUNK, CHUNK), CHUNK ** -0.5),
        'a_b_s': 1.0 + 0.1 * jax.random.normal(ks[10], (N_CHUNK_LAYERS, CHUNK_GROUPS, CHUNK), f32),
        'a_w_out': nrm(ks[11], (N_CHUNK_LAYERS, CHUNK_WIDTH, D_MODEL), CHUNK_WIDTH ** -0.5),
        'b_w_qkv': nrm(ks[12], (N_ATTN_LAYERS, D_MODEL, qkv_cols), D_MODEL ** -0.5),
        'b_q_norm': gain(ks[13], (N_ATTN_LAYERS, HEAD_DIM)),
        'b_k_norm': gain(ks[14], (N_ATTN_LAYERS, HEAD_DIM)),
        'b_w_o': nrm(ks[15], (N_ATTN_LAYERS, N_HEADS * HEAD_DIM, D_MODEL), (N_HEADS * HEAD_DIM) ** -0.5),
        'moe_w_group': nrm(ks[16], (DEPTH, D_MODEL, N_EXPERT_GROUPS), D_MODEL ** -0.5),
        'moe_b_group': nrm(ks[17], (DEPTH, N_EXPERT_GROUPS), 0.01),
        'moe_w_router': nrm(ks[18], (DEPTH, D_MODEL, N_EXPERTS), D_MODEL ** -0.5),
        'moe_b_router': nrm(ks[19], (DEPTH, N_EXPERTS), 0.01),
        'moe_w_gate': nrm(ks[20], (DEPTH, N_EXPERTS, D_MODEL, D_EXPERT), D_MODEL ** -0.5),
        'moe_w_up': nrm(ks[21], (DEPTH, N_EXPERTS, D_MODEL, D_EXPERT), D_MODEL ** -0.5),
        'moe_w_down': nrm(ks[22], (DEPTH, N_EXPERTS, D_EXPERT, D_MODEL), D_EXPERT ** -0.5),
    }


def reference(x_prompt, x_sample, cache_k, cache_v, page_table, norm1, norm2,
              a_w_in, a_g_v, a_w_s, a_b_s, a_w_out,
              b_w_qkv, b_q_norm, b_k_norm, b_w_o,
              moe_w_group, moe_b_group, moe_w_router, moe_b_router,
              moe_w_gate, moe_w_up, moe_w_down):
    hp, hs = x_prompt, x_sample
    bp, tp = hp.shape[0], hp.shape[1]
    bs, ts = hs.shape[0], hs.shape[1]
    past_len = page_table.shape[1] * cache_k.shape[2]
    pos_p = jnp.arange(tp, dtype=jnp.int32)
    pos_s = past_len + jnp.arange(ts, dtype=jnp.int32)
    new_k_p, new_v_p, new_k_s, new_v_s, new_chunk_v = [], [], [], [], []
    for i in range(DEPTH):
        j = i // N_MIXERS
        np_in = rms_norm(hp, norm1[i])
        ns_in = rms_norm(hs, norm1[i])
        if i % N_MIXERS == 0:
            mp, _ = chunk_spatial_gating(np_in, a_w_in[j], a_g_v[j], a_w_s[j], a_b_s[j], a_w_out[j])
            ms, v_rows = chunk_spatial_gating(ns_in, a_w_in[j], a_g_v[j], a_w_s[j], a_b_s[j], a_w_out[j])
            new_chunk_v.append(v_rows)
        else:
            qp, kp, vp = attn_qkv(np_in, pos_p, b_w_qkv[j], b_q_norm[j], b_k_norm[j])
            op = moba_attention(qp, kp, vp, pos_p)
            qs, ks_new, vs_new = attn_qkv(ns_in, pos_s, b_w_qkv[j], b_q_norm[j], b_k_norm[j])
            past_k = cache_k[j][page_table].reshape(bs, past_len, N_KV_HEADS, HEAD_DIM)
            past_v = cache_v[j][page_table].reshape(bs, past_len, N_KV_HEADS, HEAD_DIM)
            k_all = jnp.concatenate([past_k, ks_new], axis=1)
            v_all = jnp.concatenate([past_v, vs_new], axis=1)
            os_ = moba_attention(qs, k_all, v_all, pos_s)
            mp = op.reshape(bp, tp, N_HEADS * HEAD_DIM) @ b_w_o[j]
            ms = os_.reshape(bs, ts, N_HEADS * HEAD_DIM) @ b_w_o[j]
            new_k_p.append(kp)
            new_v_p.append(vp)
            new_k_s.append(ks_new)
            new_v_s.append(vs_new)
        hp = hp + mp
        hs = hs + ms
        hp = hp + hier_moe(rms_norm(hp, norm2[i]), moe_w_group[i], moe_b_group[i], moe_w_router[i], moe_b_router[i], moe_w_gate[i], moe_w_up[i], moe_w_down[i])
        hs = hs + hier_moe(rms_norm(hs, norm2[i]), moe_w_group[i], moe_b_group[i], moe_w_router[i], moe_b_router[i], moe_w_gate[i], moe_w_up[i], moe_w_down[i])
    return (hp, hs, jnp.stack(new_k_p), jnp.stack(new_v_p), jnp.stack(new_k_s), jnp.stack(new_v_s), jnp.stack(new_chunk_v))
```

```python
import functools

import jax
import jax.numpy as jnp
from jax import lax
from jax.experimental import pallas as pl
from jax.experimental.pallas import tpu as pltpu

F32 = jnp.float32
BF16 = jnp.bfloat16

CHUNK = 128
CHUNK_GROUPS = 4
HEAD_DIM = 128
N_HEADS = 8
N_KV_HEADS = 4
KV_GROUP = N_HEADS // N_KV_HEADS
ROT_DIM = HEAD_DIM // 4
ROPE_THETA = 500000.0
MOBA_BLOCK = 256
MOBA_TOP_K = 3
N_EXPERT_GROUPS = 4
EXPERTS_PER_GROUP = 4
N_EXPERTS = N_EXPERT_GROUPS * EXPERTS_PER_GROUP
NORM_EPS = 1e-6
NEG_INF = -1e30

LANES = 128
VMEM_LIMIT_BYTES = 56 * 1024 * 1024

_NT = (((1,), (1,)), ((), ()))


def _dot(a, b):
    return jnp.dot(a, b, preferred_element_type=F32)


def _dot_nt(a, b):
    return lax.dot_general(a, b, _NT, preferred_element_type=F32)


def _split_bf16(a):
    hi = a.astype(BF16)
    lo = (a - hi.astype(F32)).astype(BF16)
    return hi, lo


def _dot3(a, b, nt=False):
    d = _dot_nt if nt else _dot
    ah, al = _split_bf16(a)
    bh, bl = _split_bf16(b)
    return d(ah, bh) + (d(ah, bl) + d(al, bh))


def _rms(x, g):
    ms = jnp.mean(x * x, axis=-1, keepdims=True)
    return x * lax.rsqrt(ms + NORM_EPS) * g


def _gelu_tanh(x):
    c = 0.7978845608028654
    return 0.5 * x * (1.0 + jnp.tanh(c * (x + 0.044715 * (x * x * x))))


def _silu(x):
    return x * (1.0 / (1.0 + jnp.exp(-x)))


def _cparams(sem):
    return pltpu.CompilerParams(dimension_semantics=sem, vmem_limit_bytes=VMEM_LIMIT_BYTES)


def _mixer_a_kernel(h_ref, n1_ref, win_ref, gv_ref, wmix_ref, bs_ref, wout_ref, *rest,
                    seq_len, emit_v):
    if emit_v:
        out_ref, v_ref, us_scr = rest
    else:
        out_ref, us_scr = rest
    tm = h_ref.shape[0]
    width = gv_ref.shape[1]
    gdim = width // CHUNK_GROUPS
    x = h_ref[...]
    xn = _rms(x, n1_ref[...])
    z = _gelu_tanh(_dot(xn.astype(BF16), win_ref[...]))
    u = z[:, :width]
    vn = _rms(z[:, width:], gv_ref[...])
    if emit_v:
        v_ref[...] = vn
    row = lax.broadcasted_iota(jnp.int32, (CHUNK, CHUNK), 0)
    col = lax.broadcasted_iota(jnp.int32, (CHUNK, CHUNK), 1)
    mask = col <= row
    if seq_len < CHUNK:
        mask = mask & ((row // seq_len) == (col // seq_len))
    vb = vn.astype(BF16)
    for g in range(CHUNK_GROUPS):
        wm = jnp.where(mask, wmix_ref[g], 0.0).astype(BF16)
        bias = bs_ref[g]
        for c in range(tm // CHUNK):
            rows = slice(c * CHUNK, (c + 1) * CHUNK)
            cols = slice(g * gdim, (g + 1) * gdim)
            s = _dot(wm, vb[rows, cols]) + bias
            us_scr[rows, cols] = (u[rows, cols] * s).astype(BF16)
    out_ref[...] = x + _dot(us_scr[...], wout_ref[...])


def _mixer_a(h, n1, w_in, g_v, wmix, bs, w_out, *, tm, seq_len, emit_v):
    n, d = h.shape
    width = g_v.shape[1]
    const = lambda i: (0, 0)
    out_shape = [jax.ShapeDtypeStruct((n, d), F32)]
    out_specs = [pl.BlockSpec((tm, d), lambda i: (i, 0))]
    if emit_v:
        out_shape.append(jax.ShapeDtypeStruct((n, width), F32))
        out_specs.append(pl.BlockSpec((tm, width), lambda i: (i, 0)))
    res = pl.pallas_call(
        functools.partial(_mixer_a_kernel, seq_len=seq_len, emit_v=emit_v),
        grid=(n // tm,),
        in_specs=[
            pl.BlockSpec((tm, d), lambda i: (i, 0)),
            pl.BlockSpec((1, d), const),
            pl.BlockSpec((d, 2 * width), const),
            pl.BlockSpec((1, width), const),
            pl.BlockSpec((CHUNK_GROUPS, CHUNK, CHUNK), lambda i: (0, 0, 0)),
            pl.BlockSpec((CHUNK_GROUPS, CHUNK, 1), lambda i: (0, 0, 0)),
            pl.BlockSpec((width, d), const),
        ],
        out_specs=out_specs,
        out_shape=out_shape,
        scratch_shapes=[pltpu.VMEM((tm, width), BF16)],
        compiler_params=_cparams(("parallel",)),
        name="mixer_a_v" if emit_v else "mixer_a",
    )(h, n1, w_in, g_v, wmix, bs, w_out)
    return res if emit_v else res[0]


def _qkv_kernel(h_ref, n1_ref, w_ref, qn_ref, kn_ref, cos_ref, sa_ref, sb_ref,
                q_ref, k_ref, v_ref):
    xn = _rms(h_ref[...], n1_ref[...])
    qkv = _dot(xn.astype(BF16), w_ref[...])
    cos = cos_ref[...]
    sa = sa_ref[...]
    sb = sb_ref[...]
    nq = q_ref.shape[1]
    nk = k_ref.shape[1]

    def head(t, g):
        t = _rms(t, g)
        return (t * cos + pltpu.roll(t, HEAD_DIM - ROT_DIM // 2, 1) * sa
                + pltpu.roll(t, ROT_DIM // 2, 1) * sb)

    for hd in range(nq // HEAD_DIM):
        cols = slice(hd * HEAD_DIM, (hd + 1) * HEAD_DIM)
        q_ref[:, cols] = head(qkv[:, cols], qn_ref[...])
    for hd in range(nk // HEAD_DIM):
        cols = slice(hd * HEAD_DIM, (hd + 1) * HEAD_DIM)
        k_ref[:, cols] = head(qkv[:, nq + hd * HEAD_DIM: nq + (hd + 1) * HEAD_DIM], kn_ref[...])
    v_ref[...] = qkv[:, nq + nk:]


def _rope_tables(pos):
    half = ROT_DIM // 2
    inv_freq = jnp.power(ROPE_THETA, -jnp.arange(half, dtype=F32) * 2.0 / ROT_DIM)
    ang = pos.astype(F32)[:, None] * inv_freq[None, :]
    cos = jnp.cos(ang)
    sin = jnp.sin(ang)
    t = pos.shape[0]
    ones = jnp.ones((t, HEAD_DIM - ROT_DIM), F32)
    zeros = jnp.zeros((t, HEAD_DIM - ROT_DIM), F32)
    z16 = jnp.zeros((t, half), F32)
    cos_t = jnp.concatenate([cos, cos, ones], axis=1)
    sa_t = jnp.concatenate([-sin, z16, zeros], axis=1)
    sb_t = jnp.concatenate([z16, sin, zeros], axis=1)
    return cos_t, sa_t, sb_t


def _qkv(h, n1, w, qn, kn, tables, *, tm):
    n, d = h.shape
    nq = N_HEADS * HEAD_DIM
    nk = N_KV_HEADS * HEAD_DIM
    t = tables[0].shape[0]
    tiles_per_seq = t // tm
    const = lambda i: (0, 0)
    tab = pl.BlockSpec((tm, HEAD_DIM), lambda i: (i % tiles_per_seq, 0))
    return pl.pallas_call(
        _qkv_kernel,
        grid=(n // tm,),
        in_specs=[
            pl.BlockSpec((tm, d), lambda i: (i, 0)),
            pl.BlockSpec((1, d), const),
            pl.BlockSpec((d, nq + 2 * nk), const),
            pl.BlockSpec((1, HEAD_DIM), const),
            pl.BlockSpec((1, HEAD_DIM), const),
            tab, tab, tab,
        ],
        out_specs=[
            pl.BlockSpec((tm, nq), lambda i: (i, 0)),
            pl.BlockSpec((tm, nk), lambda i: (i, 0)),
            pl.BlockSpec((tm, nk), lambda i: (i, 0)),
        ],
        out_shape=[
            jax.ShapeDtypeStruct((n, nq), F32),
            jax.ShapeDtypeStruct((n, nk), F32),
            jax.ShapeDtypeStruct((n, nk), F32),
        ],
        compiler_params=_cparams(("parallel",)),
        name="qkv",
    )(h, n1, w, qn, kn, *tables)


def _attn_prompt_kernel(q_ref, k_ref, v_ref, h_ref, wo_ref, out_ref,
                        kbf, vbf, kmean, o_scr, m_scr, l_scr, acc_scr):
    blk = MOBA_BLOCK
    nblk = k_ref.shape[0] // blk
    own = pl.program_id(1)
    scale = HEAD_DIM ** -0.5

    @pl.when(own == 0)
    def _():
        kbf[...] = k_ref[...].astype(BF16)
        vbf[...] = v_ref[...].astype(BF16)
        kmean[...] = jnp.zeros_like(kmean)
        for n in range(nblk):
            kmean[n:n + 1, :] = jnp.mean(k_ref[n * blk:(n + 1) * blk, :], axis=0, keepdims=True)

    rows = KV_GROUP * blk
    nidx = lax.broadcasted_iota(jnp.int32, (LANES, rows), 0)
    lane = lax.broadcasted_iota(jnp.int32, (rows, LANES), 1)
    qrow = lax.broadcasted_iota(jnp.int32, (rows, blk), 0) % blk
    kcol = lax.broadcasted_iota(jnp.int32, (rows, blk), 1)
    causal = kcol <= qrow
    own_start = pl.multiple_of(own * blk, blk)

    for kh in range(N_KV_HEADS):
        kcols = slice(kh * HEAD_DIM, (kh + 1) * HEAD_DIM)
        q2 = jnp.concatenate(
            [q_ref[:, (KV_GROUP * kh + i) * HEAD_DIM:(KV_GROUP * kh + i + 1) * HEAD_DIM]
             for i in range(KV_GROUP)], axis=0)
        q2s = (q2 * scale).astype(BF16)

        g_t = _dot3(kmean[:, kcols], q2, nt=True)
        rank = jnp.zeros_like(g_t)
        for m in range(nblk - 1):
            gm = g_t[m:m + 1, :]
            beats = (gm > g_t) | ((gm == g_t) & (nidx > m))
            inc = jnp.where(m < own, 1.0, 0.0)
            rank = rank + jnp.where(beats, inc, 0.0)
        sel_t = jnp.where((nidx < own) & (rank < float(MOBA_TOP_K)), 1.0, 0.0)
        sel = sel_t.T

        s = _dot_nt(q2s, kbf[pl.ds(own_start, blk), kcols])
        s = jnp.where(causal, s, NEG_INF)
        m0 = jnp.max(s, axis=-1, keepdims=True)
        p = jnp.exp(s - m0)
        m_scr[...] = m0
        l_scr[...] = jnp.sum(p, axis=-1, keepdims=True)
        acc_scr[...] = _dot(p.astype(BF16), vbf[pl.ds(own_start, blk), kcols])

        def body(n, carry):
            start = pl.multiple_of(n * blk, blk)
            s = _dot_nt(q2s, kbf[pl.ds(start, blk), kcols])
            picked = jnp.max(jnp.where(lane == n, sel, 0.0), axis=1, keepdims=True)
            s = jnp.where(picked > 0.5, s, NEG_INF)
            m_prev = m_scr[...]
            m_new = jnp.maximum(m_prev, jnp.max(s, axis=-1, keepdims=True))
            alpha = jnp.exp(m_prev - m_new)
            p = jnp.exp(s - m_new)
            l_scr[...] = alpha * l_scr[...] + jnp.sum(p, axis=-1, keepdims=True)
            acc_scr[...] = alpha * acc_scr[...] + _dot(p.astype(BF16), vbf[pl.ds(start, blk), kcols])
            m_scr[...] = m_new
            return carry

        lax.fori_loop(0, own, body, 0)
        o = (acc_scr[...] / l_scr[...]).astype(BF16)
        for i in range(KV_GROUP):
            hd = KV_GROUP * kh + i
            o_scr[:, hd * HEAD_DIM:(hd + 1) * HEAD_DIM] = o[i * blk:(i + 1) * blk]

    out_ref[...] = h_ref[...] + _dot(o_scr[...], wo_ref[...])


def _attn_prompt(q, k, v, h, wo, *, batch, seq):
    blk = MOBA_BLOCK
    nqb = seq // blk
    d = h.shape[1]
    nq = q.shape[1]
    nk = k.shape[1]
    rows = KV_GROUP * blk
    return pl.pallas_call(
        _attn_prompt_kernel,
        grid=(batch, nqb),
        in_specs=[
            pl.BlockSpec((blk, nq), lambda b, i: (b * nqb + i, 0)),
            pl.BlockSpec((seq, nk), lambda b, i: (b, 0)),
            pl.BlockSpec((seq, nk), lambda b, i: (b, 0)),
            pl.BlockSpec((blk, d), lambda b, i: (b * nqb + i, 0)),
            pl.BlockSpec((nq, d), lambda b, i: (0, 0)),
        ],
        out_specs=pl.BlockSpec((blk, d), lambda b, i: (b * nqb + i, 0)),
        out_shape=jax.ShapeDtypeStruct(h.shape, F32),
        scratch_shapes=[
            pltpu.VMEM((seq, nk), BF16),
            pltpu.VMEM((seq, nk), BF16),
            pltpu.VMEM((LANES, nk), F32),
            pltpu.VMEM((blk, nq), BF16),
            pltpu.VMEM((rows, 1), F32),
            pltpu.VMEM((rows, 1), F32),
            pltpu.VMEM((rows, HEAD_DIM), F32),
        ],
        compiler_params=_cparams(("parallel", "arbitrary")),
        name="attn_prompt",
    )(q, k, v, h, wo)


def _attn_sample_kernel(pt_ref, qrows_ref, knew_ref, vnew_ref, *rest, npg, dec_seq):
    del pt_ref
    kpages = rest[:npg]
    vpages = rest[npg:2 * npg]
    out_ref = rest[2 * npg]
    s_scr, p_scr, kmean_scr, acc_scr, l_scr = rest[2 * npg + 1:]
    sweep = pl.program_id(1)
    c = pl.program_id(2)
    nchunk = pl.num_programs(2)
    page = kpages[0].shape[0]
    ckeys = npg * page
    pages_per_blk = MOBA_BLOCK // page
    blk_per_chunk = npg // pages_per_blk
    nrow = qrows_ref.shape[0]
    nblk = kmean_scr.shape[0]
    scale = HEAD_DIM ** -0.5

    @pl.when(sweep == 0)
    def _():
        kf = [kp[...] for kp in kpages]
        for i in range(blk_per_chunk):
            tot = kf[pages_per_blk * i]
            for j in range(1, pages_per_blk):
                tot = tot + kf[pages_per_blk * i + j]
            kmean_scr[pl.ds(c * blk_per_chunk + i, 1), :] = (
                jnp.sum(tot, axis=0, keepdims=True) * (1.0 / MOBA_BLOCK))
        kb = jnp.concatenate([x.astype(BF16) for x in kf], axis=0)
        s_scr[c] = _dot_nt(qrows_ref[...].astype(BF16), kb)

    @pl.when((sweep == 0) & (c == nchunk - 1))
    def _():
        qr = qrows_ref[...]
        gate = _dot3(qr, kmean_scr[...], nt=True)
        lane = lax.broadcasted_iota(jnp.int32, gate.shape, 1)
        rank = jnp.zeros_like(gate)
        for m in range(nblk):
            gm = gate[:, m:m + 1]
            beats = (gm > gate) | ((gm == gate) & (lane > m))
            rank = rank + jnp.where(beats, 1.0, 0.0)
        sel = jnp.where(rank < float(MOBA_TOP_K), 1.0, 0.0).astype(BF16)

        t_of_row = lax.broadcasted_iota(jnp.int32, (nrow, 1), 0) % dec_seq
        knew = knew_ref[...]
        vnew = vnew_ref[...]
        s_own = [jnp.sum(qr * knew[t:t + 1, :], axis=-1, keepdims=True) * scale
                 for t in range(dec_seq)]
        ok_own = [t_of_row >= t for t in range(dec_seq)]

        nkey_blk = lax.broadcasted_iota(jnp.int32, (nblk, ckeys), 1) // MOBA_BLOCK
        blk_row = lax.broadcasted_iota(jnp.int32, (nblk, ckeys), 0)
        nch = s_scr.shape[0]
        masks = []
        m = jnp.full((nrow, 1), NEG_INF, F32)
        for cc in range(nch):
            expand = jnp.where(nkey_blk + cc * blk_per_chunk == blk_row, 1.0, 0.0).astype(BF16)
            mk = _dot(sel, expand) > 0.5
            masks.append(mk)
            sc = jnp.where(mk, s_scr[cc] * scale, NEG_INF)
            m = jnp.maximum(m, jnp.max(sc, axis=-1, keepdims=True))
        for t in range(dec_seq):
            m = jnp.maximum(m, jnp.where(ok_own[t], s_own[t], NEG_INF))
        l = jnp.zeros((nrow, 1), F32)
        for cc in range(nch):
            p = jnp.where(masks[cc], jnp.exp(s_scr[cc] * scale - m), 0.0)
            l = l + jnp.sum(p, axis=-1, keepdims=True)
            p_scr[cc] = p.astype(BF16)
        acc = jnp.zeros(acc_scr.shape, F32)
        for t in range(dec_seq):
            p_t = jnp.where(ok_own[t], jnp.exp(s_own[t] - m), 0.0)
            l = l + p_t
            acc = acc + p_t * vnew[t:t + 1, :]
        acc_scr[...] = acc
        l_scr[...] = l

    @pl.when(sweep == 1)
    def _():
        vb = jnp.concatenate([vp[...].astype(BF16) for vp in vpages], axis=0)
        acc_scr[...] += _dot(p_scr[c], vb)

    @pl.when((sweep == 1) & (c == nchunk - 1))
    def _():
        o = acc_scr[...] / l_scr[...]
        kv_of_row = (lax.broadcasted_iota(jnp.int32, (nrow, HEAD_DIM), 0) // dec_seq) // KV_GROUP
        out = jnp.zeros((nrow, HEAD_DIM), F32)
        for kh in range(N_KV_HEADS):
            out = out + jnp.where(kv_of_row == kh, o[:, kh * HEAD_DIM:(kh + 1) * HEAD_DIM], 0.0)
        out_ref[...] = out


def _attn_sample(q, k_new, v_new, cache_k, cache_v, page_table, layer, *, n_pool, batch, dec_seq,
                 npg=8):
    nk = N_KV_HEADS * HEAD_DIM
    n_pages = page_table.shape[1]
    page = cache_k.shape[1]
    assert (n_pages * page) % MOBA_BLOCK == 0 and dec_seq <= MOBA_BLOCK
    assert n_pages % npg == 0 and MOBA_BLOCK % page == 0 and npg % (MOBA_BLOCK // page) == 0
    nchunk = n_pages // npg
    nblk = n_pages * page // MOBA_BLOCK
    nrow = N_HEADS * dec_seq
    qh = q.reshape(batch, dec_seq, N_HEADS, HEAD_DIM).transpose(0, 2, 1, 3)
    kv_of_head = jnp.arange(N_HEADS) // KV_GROUP
    onehot = kv_of_head[:, None] == jnp.arange(N_KV_HEADS)[None, :]
    qrows = jnp.where(onehot[None, :, None, :, None], qh[:, :, :, None, :], 0.0)
    qrows = qrows.reshape(batch, nrow, nk)
    base = layer * n_pool

    def kmap(i):
        return lambda b, s, c, pt: (base + pt[b, jnp.where(s == 0, c, nchunk - 1) * npg + i], 0, 0)

    def vmap_(i):
        return lambda b, s, c, pt: (base + pt[b, jnp.where(s == 0, 0, c) * npg + i], 0, 0)

    per_b = lambda b, s, c, pt: (b, 0, 0)
    grid_spec = pltpu.PrefetchScalarGridSpec(
        num_scalar_prefetch=1,
        grid=(batch, 2, nchunk),
        in_specs=(
            [pl.BlockSpec((None, nrow, nk), per_b),
             pl.BlockSpec((None, dec_seq, nk), per_b),
             pl.BlockSpec((None, dec_seq, nk), per_b)]
            + [pl.BlockSpec((None, page, nk), kmap(i)) for i in range(npg)]
            + [pl.BlockSpec((None, page, nk), vmap_(i)) for i in range(npg)]
        ),
        out_specs=pl.BlockSpec((None, nrow, HEAD_DIM), per_b),
        scratch_shapes=[
            pltpu.VMEM((nchunk, nrow, npg * page), F32),
            pltpu.VMEM((nchunk, nrow, npg * page), BF16),
            pltpu.VMEM((nblk, nk), F32),
            pltpu.VMEM((nrow, nk), F32),
            pltpu.VMEM((nrow, 1), F32),
        ],
    )
    o = pl.pallas_call(
        functools.partial(_attn_sample_kernel, npg=npg, dec_seq=dec_seq),
        grid_spec=grid_spec,
        out_shape=jax.ShapeDtypeStruct((batch, nrow, HEAD_DIM), F32),
        compiler_params=_cparams(("parallel", "arbitrary", "arbitrary")),
        name="attn_sample",
    )(page_table, qrows, k_new.reshape(batch, dec_seq, nk), v_new.reshape(batch, dec_seq, nk),
      *([cache_k] * npg), *([cache_v] * npg))
    o = o.reshape(batch, N_HEADS, dec_seq, HEAD_DIM).transpose(0, 2, 1, 3)
    return o.reshape(batch * dec_seq, N_HEADS * HEAD_DIM)


def _oproj_kernel(o_ref, h_ref, wo_ref, out_ref):
    out_ref[...] = h_ref[...] + _dot(o_ref[...].astype(BF16), wo_ref[...])


def _oproj(o, h, wo):
    return pl.pallas_call(
        _oproj_kernel,
        out_shape=jax.ShapeDtypeStruct(h.shape, F32),
        compiler_params=pltpu.CompilerParams(vmem_limit_bytes=VMEM_LIMIT_BYTES),
        name="oproj",
    )(o, h, wo)


def _moe_kernel(h_ref, n2_ref, wr_ref, br_ref, wg_ref, wu_ref, wd_ref, out_ref,
                xn_scr, gates_scr, acc_scr):
    g = pl.program_id(1)
    ngroups = pl.num_programs(1)
    tm = h_ref.shape[0]
    lane = lax.broadcasted_iota(jnp.int32, (tm, LANES), 1)
    first_expert_lane = N_EXPERT_GROUPS

    @pl.when(g == 0)
    def _():
        x = h_ref[...]
        xn = _rms(x, n2_ref[...])
        xn_scr[...] = xn.astype(BF16)
        acc_scr[...] = x
        logits = _dot3(xn, wr_ref[...]) + br_ref[...]
        is_group = lane < N_EXPERT_GROUPS
        gl = jnp.where(is_group, logits, NEG_INF)
        ge = jnp.where(is_group, jnp.exp(gl - jnp.max(gl, axis=-1, keepdims=True)), 0.0)
        gprob = ge / jnp.sum(ge, axis=-1, keepdims=True)
        g_w = jnp.max(gprob, axis=-1, keepdims=True)
        g_idx = jnp.min(jnp.where(is_group & (gprob == g_w), lane, LANES), axis=-1, keepdims=True)
        lo = first_expert_lane + g_idx * EXPERTS_PER_GROUP
        in_group = (lane >= lo) & (lane < lo + EXPERTS_PER_GROUP)
        el = jnp.where(in_group, logits, NEG_INF)
        ee = jnp.where(in_group, jnp.exp(el - jnp.max(el, axis=-1, keepdims=True)), 0.0)
        eprob = jnp.where(in_group, ee / jnp.sum(ee, axis=-1, keepdims=True), -1.0)
        p1 = jnp.max(eprob, axis=-1, keepdims=True)
        i1 = jnp.min(jnp.where(eprob == p1, lane, LANES), axis=-1, keepdims=True)
        rest = jnp.where(lane == i1, -1.0, eprob)
        p2 = jnp.max(rest, axis=-1, keepdims=True)
        i2 = jnp.min(jnp.where(rest == p2, lane, LANES), axis=-1, keepdims=True)
        denom = p1 + p2
        w1 = p1 / denom * g_w
        w2 = p2 / denom * g_w
        gates_scr[...] = jnp.where(lane == i1, w1, 0.0) + jnp.where(lane == i2, w2, 0.0)

    xb = xn_scr[...]
    gates = gates_scr[...]
    acc = acc_scr[...]
    for e in range(EXPERTS_PER_GROUP):
        a = _dot(xb, wg_ref[e])
        b = _dot(xb, wu_ref[e])
        tgt = first_expert_lane + g * EXPERTS_PER_GROUP + e
        gate = jnp.sum(jnp.where(lane == tgt, gates, 0.0), axis=-1, keepdims=True)
        hid = _silu(a) * b * gate
        acc = acc + _dot(hid.astype(BF16), wd_ref[e])
    acc_scr[...] = acc

    @pl.when(g == ngroups - 1)
    def _():
        out_ref[...] = acc_scr[...]


def _moe(h, n2, wr, br, wg, wu, wd, *, tm):
    n, d = h.shape
    de = wg.shape[2]
    epg = EXPERTS_PER_GROUP
    const = lambda i, g: (0, 0)
    return pl.pallas_call(
        _moe_kernel,
        grid=(n // tm, N_EXPERT_GROUPS),
        in_specs=[
            pl.BlockSpec((tm, d), lambda i, g: (i, 0)),
            pl.BlockSpec((1, d), const),
            pl.BlockSpec((d, LANES), const),
            pl.BlockSpec((1, LANES), const),
            pl.BlockSpec((epg, d, de), lambda i, g: (g, 0, 0)),
            pl.BlockSpec((epg, d, de), lambda i, g: (g, 0, 0)),
            pl.BlockSpec((epg, de, d), lambda i, g: (g, 0, 0)),
        ],
        out_specs=pl.BlockSpec((tm, d), lambda i, g: (i, 0)),
        out_shape=jax.ShapeDtypeStruct((n, d), F32),
        scratch_shapes=[
            pltpu.VMEM((tm, d), BF16),
            pltpu.VMEM((tm, LANES), F32),
            pltpu.VMEM((tm, d), F32),
        ],
        compiler_params=_cparams(("parallel", "arbitrary")),
        name="moe",
    )(h, n2, wr, br, wg, wu, wd)


def kernel(x_prompt, x_sample, cache_k, cache_v, page_table, norm1, norm2, a_w_in, a_g_v, a_w_s, a_b_s, a_w_out, b_w_qkv, b_q_norm, b_k_norm, b_w_o, moe_w_group, moe_b_group, moe_w_router, moe_b_router, moe_w_gate, moe_w_up, moe_w_down):
    bp, tp, d = x_prompt.shape
    bs, ts, _ = x_sample.shape
    depth = norm1.shape[0]
    n_attn, n_pool, page = cache_k.shape[0], cache_k.shape[1], cache_k.shape[2]
    past_len = page_table.shape[1] * page
    nk = N_KV_HEADS * HEAD_DIM
    assert CHUNK % ts == 0 and (bs * ts) % CHUNK == 0 and tp % MOBA_BLOCK == 0

    hp = x_prompt.reshape(bp * tp, d)
    hs = x_sample.reshape(bs * ts, d)
    ck = cache_k.reshape(n_attn * n_pool, page, nk)
    cv = cache_v.reshape(n_attn * n_pool, page, nk)
    tab_p = _rope_tables(jnp.arange(tp, dtype=jnp.int32))
    pos_s = past_len + jnp.arange(ts, dtype=jnp.int32)
    tab_s = tuple(jnp.tile(t, (bs, 1)) for t in _rope_tables(pos_s))

    pad = LANES - N_EXPERT_GROUPS - N_EXPERTS
    wr_all = jnp.concatenate(
        [moe_w_group, moe_w_router, jnp.zeros((depth, d, pad), F32)], axis=2)
    br_all = jnp.concatenate(
        [moe_b_group, moe_b_router, jnp.zeros((depth, pad), F32)], axis=1)[:, None, :]

    new_k_p, new_v_p, new_k_s, new_v_s, new_chunk_v = [], [], [], [], []
    reps = CHUNK // ts
    for i in range(depth):
        j = i // 2
        n1 = norm1[i][None, :]
        if i % 2 == 0:
            w_in = a_w_in[j].astype(BF16)
            w_out = a_w_out[j].astype(BF16)
            g_v = a_g_v[j][None, :]
            hp = _mixer_a(hp, n1, w_in, g_v, a_w_s[j], a_b_s[j][:, :, None], w_out,
                          tm=512, seq_len=CHUNK, emit_v=False)
            wmix_s = jnp.tile(a_w_s[j][:, :ts, :ts], (1, reps, reps))
            bs_s = jnp.tile(a_b_s[j][:, :ts], (1, reps))[:, :, None]
            hs, v_rows = _mixer_a(hs, n1, w_in, g_v, wmix_s, bs_s, w_out,
                                  tm=CHUNK, seq_len=ts, emit_v=True)
            new_chunk_v.append(v_rows.reshape(bs, ts, -1))
        else:
            w_qkv = b_w_qkv[j].astype(BF16)
            w_o = b_w_o[j].astype(BF16)
            qn = b_q_norm[j][None, :]
            kn = b_k_norm[j][None, :]
            qp, kp, vp = _qkv(hp, n1, w_qkv, qn, kn, tab_p, tm=512)
            qs, ks_new, vs_new = _qkv(hs, n1, w_qkv, qn, kn, tab_s, tm=bs * ts)
            hp = _attn_prompt(qp, kp, vp, hp, w_o, batch=bp, seq=tp)
            o_s = _attn_sample(qs, ks_new, vs_new, ck, cv, page_table, j, n_pool=n_pool,
                               batch=bs, dec_seq=ts)
            hs = _oproj(o_s, hs, w_o)
            new_k_p.append(kp.reshape(bp, tp, N_KV_HEADS, HEAD_DIM))
            new_v_p.append(vp.reshape(bp, tp, N_KV_HEADS, HEAD_DIM))
            new_k_s.append(ks_new.reshape(bs, ts, N_KV_HEADS, HEAD_DIM))
            new_v_s.append(vs_new.reshape(bs, ts, N_KV_HEADS, HEAD_DIM))
        n2 = norm2[i][None, :]
        wg = moe_w_gate[i].astype(BF16)
        wu = moe_w_up[i].astype(BF16)
        wd = moe_w_down[i].astype(BF16)
        hp = _moe(hp, n2, wr_all[i], br_all[i], wg, wu, wd, tm=min(1024, bp * tp))
        hs = _moe(hs, n2, wr_all[i], br_all[i], wg, wu, wd, tm=bs * ts)
    return (hp.reshape(bp, tp, d), hs.reshape(bs, ts, d),
            jnp.stack(new_k_p), jnp.stack(new_v_p), jnp.stack(new_k_s), jnp.stack(new_v_s),
            jnp.stack(new_chunk_v))
```

```python
import functools

import jax
import jax.numpy as jnp
from jax import lax
from jax.experimental import pallas as pl
from jax.experimental.pallas import tpu as pltpu

F32 = jnp.float32
BF16 = jnp.bfloat16

CHUNK = 128
CHUNK_GROUPS = 4
HEAD_DIM = 128
N_HEADS = 8
N_KV_HEADS = 4
KV_GROUP = N_HEADS // N_KV_HEADS
ROT_DIM = HEAD_DIM // 4
ROPE_THETA = 500000.0
MOBA_BLOCK = 256
MOBA_TOP_K = 3
N_EXPERT_GROUPS = 4
EXPERTS_PER_GROUP = 4
N_EXPERTS = N_EXPERT_GROUPS * EXPERTS_PER_GROUP
NORM_EPS = 1e-6
NEG_INF = -1e30
LOG2E = 1.4426950408889634
SPAN_BLOCKS = 2

SUBLANES = 8
LANES = 128
VMEM_LIMIT_BYTES = 56 * 1024 * 1024

_NT = (((1,), (1,)), ((), ()))


def _dot(a, b):
    return jnp.dot(a, b, preferred_element_type=F32)


def _dot_nt(a, b):
    return lax.dot_general(a, b, _NT, preferred_element_type=F32)


def _split_bf16(a):
    hi = a.astype(BF16)
    lo = (a - hi.astype(F32)).astype(BF16)
    return hi, lo


def _dot3(a, b, nt=False):
    d = _dot_nt if nt else _dot
    ah, al = _split_bf16(a)
    bh, bl = _split_bf16(b)
    return d(ah, bh) + (d(ah, bl) + d(al, bh))


def _rms(x, g):
    ms = jnp.mean(x * x, axis=-1, keepdims=True)
    return x * lax.rsqrt(ms + NORM_EPS) * g


def _gelu_tanh(x):
    c = 0.7978845608028654
    return 0.5 * x * (1.0 + jnp.tanh(c * (x + 0.044715 * (x * x * x))))


def _silu(x):
    return x * (1.0 / (1.0 + jnp.exp(-x)))


def _cparams(sem):
    return pltpu.CompilerParams(dimension_semantics=sem, vmem_limit_bytes=VMEM_LIMIT_BYTES)


def _mixer_a_kernel(h_ref, n1_ref, win_ref, gv_ref, wmix_ref, bs_ref, wout_ref, *rest,
                    seq_len, emit_v):
    if emit_v:
        out_ref, v_ref, us_scr = rest
    else:
        out_ref, us_scr = rest
    tm = h_ref.shape[0]
    width = gv_ref.shape[1]
    gdim = width // CHUNK_GROUPS
    x = h_ref[...]
    xn = _rms(x, n1_ref[...])
    z = _gelu_tanh(_dot(xn.astype(BF16), win_ref[...]))
    u = z[:, :width]
    vn = _rms(z[:, width:], gv_ref[...])
    if emit_v:
        v_ref[...] = vn
    row = lax.broadcasted_iota(jnp.int32, (CHUNK, CHUNK), 0)
    col = lax.broadcasted_iota(jnp.int32, (CHUNK, CHUNK), 1)
    mask = col <= row
    if seq_len < CHUNK:
        mask = mask & ((row // seq_len) == (col // seq_len))
    vb = vn.astype(BF16)
    for g in range(CHUNK_GROUPS):
        wm = jnp.where(mask, wmix_ref[g], 0.0).astype(BF16)
        bias = bs_ref[g]
        for c in range(tm // CHUNK):
            rows = slice(c * CHUNK, (c + 1) * CHUNK)
            cols = slice(g * gdim, (g + 1) * gdim)
            s = _dot(wm, vb[rows, cols]) + bias
            us_scr[rows, cols] = (u[rows, cols] * s).astype(BF16)
    out_ref[...] = x + _dot(us_scr[...], wout_ref[...])


def _mixer_a(h, n1, w_in, g_v, wmix, bs, w_out, *, tm, seq_len, emit_v):
    n, d = h.shape
    width = g_v.shape[1]
    const = lambda i: (0, 0)
    out_shape = [jax.ShapeDtypeStruct((n, d), F32)]
    out_specs = [pl.BlockSpec((tm, d), lambda i: (i, 0))]
    if emit_v:
        out_shape.append(jax.ShapeDtypeStruct((n, width), F32))
        out_specs.append(pl.BlockSpec((tm, width), lambda i: (i, 0)))
    res = pl.pallas_call(
        functools.partial(_mixer_a_kernel, seq_len=seq_len, emit_v=emit_v),
        grid=(n // tm,),
        in_specs=[
            pl.BlockSpec((tm, d), lambda i: (i, 0)),
            pl.BlockSpec((1, d), const),
            pl.BlockSpec((d, 2 * width), const),
            pl.BlockSpec((1, width), const),
            pl.BlockSpec((CHUNK_GROUPS, CHUNK, CHUNK), lambda i: (0, 0, 0)),
            pl.BlockSpec((CHUNK_GROUPS, CHUNK, 1), lambda i: (0, 0, 0)),
            pl.BlockSpec((width, d), const),
        ],
        out_specs=out_specs,
        out_shape=out_shape,
        scratch_shapes=[pltpu.VMEM((tm, width), BF16)],
        compiler_params=_cparams(("parallel",)),
        name="mixer_a_v" if emit_v else "mixer_a",
    )(h, n1, w_in, g_v, wmix, bs, w_out)
    return res if emit_v else res[0]


def _qkv_kernel(h_ref, n1_ref, w_ref, qn_ref, kn_ref, cos_ref, sa_ref, sb_ref,
                q_ref, k_ref, v_ref):
    xn = _rms(h_ref[...], n1_ref[...])
    qkv = _dot(xn.astype(BF16), w_ref[...])
    cos = cos_ref[...]
    sa = sa_ref[...]
    sb = sb_ref[...]
    nq = q_ref.shape[1]
    nk = k_ref.shape[1]

    def head(t, g):
        t = _rms(t, g)
        return (t * cos + pltpu.roll(t, HEAD_DIM - ROT_DIM // 2, 1) * sa
                + pltpu.roll(t, ROT_DIM // 2, 1) * sb)

    for hd in range(nq // HEAD_DIM):
        cols = slice(hd * HEAD_DIM, (hd + 1) * HEAD_DIM)
        q_ref[:, cols] = head(qkv[:, cols], qn_ref[...])
    for hd in range(nk // HEAD_DIM):
        cols = slice(hd * HEAD_DIM, (hd + 1) * HEAD_DIM)
        k_ref[:, cols] = head(qkv[:, nq + hd * HEAD_DIM: nq + (hd + 1) * HEAD_DIM], kn_ref[...])
    v_ref[...] = qkv[:, nq + nk:]


def _rope_tables(pos):
    half = ROT_DIM // 2
    inv_freq = jnp.power(ROPE_THETA, -jnp.arange(half, dtype=F32) * 2.0 / ROT_DIM)
    ang = pos.astype(F32)[:, None] * inv_freq[None, :]
    cos = jnp.cos(ang)
    sin = jnp.sin(ang)
    t = pos.shape[0]
    ones = jnp.ones((t, HEAD_DIM - ROT_DIM), F32)
    zeros = jnp.zeros((t, HEAD_DIM - ROT_DIM), F32)
    z16 = jnp.zeros((t, half), F32)
    cos_t = jnp.concatenate([cos, cos, ones], axis=1)
    sa_t = jnp.concatenate([-sin, z16, zeros], axis=1)
    sb_t = jnp.concatenate([z16, sin, zeros], axis=1)
    return cos_t, sa_t, sb_t


def _qkv(h, n1, w, qn, kn, tables, *, tm):
    n, d = h.shape
    nq = N_HEADS * HEAD_DIM
    nk = N_KV_HEADS * HEAD_DIM
    t = tables[0].shape[0]
    tiles_per_seq = t // tm
    const = lambda i: (0, 0)
    tab = pl.BlockSpec((tm, HEAD_DIM), lambda i: (i % tiles_per_seq, 0))
    return pl.pallas_call(
        _qkv_kernel,
        grid=(n // tm,),
        in_specs=[
            pl.BlockSpec((tm, d), lambda i: (i, 0)),
            pl.BlockSpec((1, d), const),
            pl.BlockSpec((d, nq + 2 * nk), const),
            pl.BlockSpec((1, HEAD_DIM), const),
            pl.BlockSpec((1, HEAD_DIM), const),
            tab, tab, tab,
        ],
        out_specs=[
            pl.BlockSpec((tm, nq), lambda i: (i, 0)),
            pl.BlockSpec((tm, nk), lambda i: (i, 0)),
            pl.BlockSpec((tm, nk), lambda i: (i, 0)),
        ],
        out_shape=[
            jax.ShapeDtypeStruct((n, nq), F32),
            jax.ShapeDtypeStruct((n, nk), F32),
            jax.ShapeDtypeStruct((n, nk), F32),
        ],
        compiler_params=_cparams(("parallel",)),
        name="qkv",
    )(h, n1, w, qn, kn, *tables)


def _attn_prompt_kernel(q_ref, k_ref, v_ref, h_ref, wo_ref, out_ref,
                        kbf, vtb, kmean, s_scr, p_scr, o_scr):
    blk = MOBA_BLOCK
    nblk = k_ref.shape[0] // blk
    own = pl.program_id(1)
    scale = HEAD_DIM ** -0.5

    @pl.when(own == 0)
    def _():
        kbf[...] = k_ref[...].astype(BF16)
        for n in range(nblk):
            rows = slice(n * blk, (n + 1) * blk)
            kmean[n:n + 1, :] = jnp.mean(k_ref[rows, :], axis=0, keepdims=True)
            vtb[:, rows] = v_ref[rows, :].T.astype(BF16)

    cols = KV_GROUP * blk
    nidx = lax.broadcasted_iota(jnp.int32, (nblk, cols), 0)
    kq_diff = (lax.broadcasted_iota(jnp.int32, (blk, cols), 0)
               - lax.broadcasted_iota(jnp.int32, (blk, cols), 1) % blk)

    for kh in range(N_KV_HEADS):
        kcols = slice(kh * HEAD_DIM, (kh + 1) * HEAD_DIM)
        q2 = jnp.concatenate(
            [q_ref[:, (KV_GROUP * kh + i) * HEAD_DIM:(KV_GROUP * kh + i + 1) * HEAD_DIM]
             for i in range(KV_GROUP)], axis=0)
        q2s = (q2 * (scale * LOG2E)).astype(BF16)

        g_t = _dot3(kmean[:, kcols], q2, nt=True)
        rank = jnp.zeros_like(g_t)
        for m in range(nblk - 1):
            gm = g_t[m:m + 1, :]
            beats = (gm > g_t) | ((gm == g_t) & (nidx > m))
            inc = jnp.where(m < own, 1.0, 0.0)
            rank = rank + jnp.where(beats, inc, 0.0)
        sel_t = jnp.where(nidx == own, 1.0,
                          jnp.where((nidx < own) & (rank < float(MOBA_TOP_K)), 1.0, 0.0))

        def attend(nb, kh=kh, kcols=kcols, q2s=q2s, sel_t=sel_t):
            m = jnp.full((1, cols), NEG_INF, F32)
            for n in range(nb):
                rows = slice(n * blk, (n + 1) * blk)
                s = _dot_nt(kbf[rows, kcols], q2s)
                s = jnp.where(sel_t[n:n + 1, :] > 0.5, s, NEG_INF)
                if n >= nb - SPAN_BLOCKS:
                    s = jnp.where(kq_diff <= (own - n) * blk, s, NEG_INF)
                s_scr[rows, :] = s
                m = jnp.maximum(m, jnp.max(s, axis=0, keepdims=True))
            l = jnp.zeros((1, cols), F32)
            for n in range(nb):
                rows = slice(n * blk, (n + 1) * blk)
                p = jnp.exp2(s_scr[rows, :] - m)
                l = l + jnp.sum(p, axis=0, keepdims=True)
                p_scr[rows, :] = p.astype(BF16)
            acc = _dot(vtb[kcols, :nb * blk], p_scr[:nb * blk, :])
            o = (acc / l).T.astype(BF16)
            for i in range(KV_GROUP):
                hd = KV_GROUP * kh + i
                o_scr[:, hd * HEAD_DIM:(hd + 1) * HEAD_DIM] = o[i * blk:(i + 1) * blk]

        for nb in range(SPAN_BLOCKS, nblk + SPAN_BLOCKS, SPAN_BLOCKS):
            pl.when((own >= nb - SPAN_BLOCKS) & (own < nb))(functools.partial(attend, min(nb, nblk)))

    out_ref[...] = h_ref[...] + _dot(o_scr[...], wo_ref[...])


def _attn_prompt(q, k, v, h, wo, *, batch, seq):
    blk = MOBA_BLOCK
    nqb = seq // blk
    d = h.shape[1]
    nq = q.shape[1]
    nk = k.shape[1]
    cols = KV_GROUP * blk
    return pl.pallas_call(
        _attn_prompt_kernel,
        grid=(batch, nqb),
        in_specs=[
            pl.BlockSpec((blk, nq), lambda b, i: (b * nqb + i, 0)),
            pl.BlockSpec((seq, nk), lambda b, i: (b, 0)),
            pl.BlockSpec((seq, nk), lambda b, i: (b, 0)),
            pl.BlockSpec((blk, d), lambda b, i: (b * nqb + i, 0)),
            pl.BlockSpec((nq, d), lambda b, i: (0, 0)),
        ],
        out_specs=pl.BlockSpec((blk, d), lambda b, i: (b * nqb + i, 0)),
        out_shape=jax.ShapeDtypeStruct(h.shape, F32),
        scratch_shapes=[
            pltpu.VMEM((seq, nk), BF16),
            pltpu.VMEM((nk, seq), BF16),
            pltpu.VMEM((nqb, nk), F32),
            pltpu.VMEM((seq, cols), F32),
            pltpu.VMEM((seq, cols), BF16),
            pltpu.VMEM((blk, nq), BF16),
        ],
        compiler_params=_cparams(("parallel", "arbitrary")),
        name="attn_prompt",
    )(q, k, v, h, wo)


def _attn_sample_kernel(pt_ref, qd_ref, knew_ref, vnew_ref, *rest, npg, dec_seq):
    del pt_ref
    kpages = rest[:npg]
    vpages = rest[npg:2 * npg]
    out_ref = rest[2 * npg]
    s_scr, p_scr, kmean_scr, acc_scr, l_scr = rest[2 * npg + 1:]
    sweep = pl.program_id(1)
    c = pl.program_id(2)
    nchunk = pl.num_programs(2)
    prow = kpages[0].shape[0]
    bcols = MOBA_BLOCK * N_KV_HEADS
    pages_per_blk = bcols // prow
    blk_per_chunk = npg // pages_per_blk
    nrow = qd_ref.shape[0]
    nl = kmean_scr.shape[0]
    nblk = nl // SUBLANES
    scale = HEAD_DIM ** -0.5

    @pl.when(sweep == 0)
    def _():
        kf = [kp[...] for kp in kpages]
        for i in range(blk_per_chunk):
            tot = kf[pages_per_blk * i]
            for j in range(1, pages_per_blk):
                tot = tot + kf[pages_per_blk * i + j]
            part = jnp.sum(tot.reshape(prow // SUBLANES, SUBLANES, HEAD_DIM), axis=0)
            part = (part + pltpu.roll(part, N_KV_HEADS, 0)) * (1.0 / MOBA_BLOCK)
            row0 = pl.multiple_of((c * blk_per_chunk + i) * SUBLANES, SUBLANES)
            kmean_scr[pl.ds(row0, SUBLANES), :] = part
        kb = jnp.concatenate([x.astype(BF16) for x in kf], axis=0)
        s_scr[c] = _dot_nt(qd_ref[...].astype(BF16), kb)

    @pl.when((sweep == 0) & (c == nchunk - 1))
    def _():
        qd = qd_ref[...]
        g = _dot3(qd, kmean_scr[...], nt=True)
        lane = lax.broadcasted_iota(jnp.int32, (nrow, nl), 1)
        kv_r = (lax.broadcasted_iota(jnp.int32, (nrow, nl), 0) // dec_seq) // KV_GROUP
        rowsel = [jnp.where(kv_r == kh, 1.0, 0.0) for kh in range(N_KV_HEADS)]
        rank = jnp.zeros_like(g)
        for m in range(nblk):
            for kh in range(N_KV_HEADS):
                pos = m * SUBLANES + kh
                col = g[:, pos:pos + 1]
                beats = (col > g) | ((col == g) & (lane > pos))
                rank = rank + jnp.where(beats, rowsel[kh], 0.0)
        sel = jnp.where((lane % SUBLANES) == kv_r,
                        jnp.where(rank < float(MOBA_TOP_K), 1.0, 0.0), 0.0)
        blkid = lane // SUBLANES

        kv_c = (lax.broadcasted_iota(jnp.int32, (nrow, 1), 0) // dec_seq) // KV_GROUP
        t_row = lax.broadcasted_iota(jnp.int32, (nrow, 1), 0) % dec_seq

        def rows_for(x, t):
            out = jnp.zeros((nrow, HEAD_DIM), F32)
            for kh in range(N_KV_HEADS):
                r = t * N_KV_HEADS + kh
                out = out + jnp.where(kv_c == kh, x[r:r + 1, :], 0.0)
            return out

        knew = knew_ref[...]
        vnew = vnew_ref[...]
        s_own = [jnp.sum(qd * rows_for(knew, t), axis=-1, keepdims=True) * scale
                 for t in range(dec_seq)]
        ok_own = [t_row >= t for t in range(dec_seq)]

        kh_match = (lax.broadcasted_iota(jnp.int32, (nrow, bcols), 1) % N_KV_HEADS) == (
            (lax.broadcasted_iota(jnp.int32, (nrow, bcols), 0) // dec_seq) // KV_GROUP)
        m = jnp.full((nrow, 1), NEG_INF, F32)
        for t in range(dec_seq):
            m = jnp.maximum(m, jnp.where(ok_own[t], s_own[t], NEG_INF))
        for n in range(nblk):
            cc, off = divmod(n, blk_per_chunk)
            cs = slice(off * bcols, (off + 1) * bcols)
            pick = jnp.max(jnp.where(blkid == n, sel, 0.0), axis=1, keepdims=True)
            sc = jnp.where(pick > 0.5, s_scr[cc, :, cs] * scale, NEG_INF)
            sc = jnp.where(kh_match, sc, NEG_INF)
            s_scr[cc, :, cs] = sc
            m = jnp.maximum(m, jnp.max(sc, axis=-1, keepdims=True))
        l = jnp.zeros((nrow, 1), F32)
        for n in range(nblk):
            cc, off = divmod(n, blk_per_chunk)
            cs = slice(off * bcols, (off + 1) * bcols)
            p = jnp.exp(s_scr[cc, :, cs] - m)
            l = l + jnp.sum(p, axis=-1, keepdims=True)
            p_scr[cc, :, cs] = p.astype(BF16)
        acc = jnp.zeros((nrow, HEAD_DIM), F32)
        for t in range(dec_seq):
            p_t = jnp.where(ok_own[t], jnp.exp(s_own[t] - m), 0.0)
            l = l + p_t
            acc = acc + p_t * rows_for(vnew, t)
        acc_scr[...] = acc
        l_scr[...] = l

    @pl.when(sweep == 1)
    def _():
        vb = jnp.concatenate([vp[...].astype(BF16) for vp in vpages], axis=0)
        acc_scr[...] += _dot(p_scr[c], vb)

    @pl.when((sweep == 1) & (c == nchunk - 1))
    def _():
        out_ref[...] = acc_scr[...] / l_scr[...]


def _attn_sample(q, k_new, v_new, cache_k, cache_v, page_table, layer, *, n_pool, batch, dec_seq,
                 npg=8):
    n_pages = page_table.shape[1]
    prow = cache_k.shape[1]
    page = prow // N_KV_HEADS
    assert (n_pages * page) % MOBA_BLOCK == 0 and dec_seq <= MOBA_BLOCK
    assert n_pages % npg == 0 and MOBA_BLOCK % page == 0 and npg % (MOBA_BLOCK // page) == 0
    assert SUBLANES == 2 * N_KV_HEADS and prow % SUBLANES == 0
    nchunk = n_pages // npg
    nblk = n_pages * page // MOBA_BLOCK
    nrow = N_HEADS * dec_seq
    qd = q.reshape(batch, dec_seq, N_HEADS, HEAD_DIM).transpose(0, 2, 1, 3)
    qd = qd.reshape(batch, nrow, HEAD_DIM)
    base = layer * n_pool

    def kmap(i):
        return lambda b, s, c, pt: (base + pt[b, jnp.where(s == 0, c, nchunk - 1) * npg + i], 0, 0)

    def vmap_(i):
        return lambda b, s, c, pt: (base + pt[b, jnp.where(s == 0, 0, c) * npg + i], 0, 0)

    per_b = lambda b, s, c, pt: (b, 0, 0)
    new_rows = dec_seq * N_KV_HEADS
    grid_spec = pltpu.PrefetchScalarGridSpec(
        num_scalar_prefetch=1,
        grid=(batch, 2, nchunk),
        in_specs=(
            [pl.BlockSpec((None, nrow, HEAD_DIM), per_b),
             pl.BlockSpec((None, new_rows, HEAD_DIM), per_b),
             pl.BlockSpec((None, new_rows, HEAD_DIM), per_b)]
            + [pl.BlockSpec((None, prow, HEAD_DIM), kmap(i)) for i in range(npg)]
            + [pl.BlockSpec((None, prow, HEAD_DIM), vmap_(i)) for i in range(npg)]
        ),
        out_specs=pl.BlockSpec((None, nrow, HEAD_DIM), per_b),
        scratch_shapes=[
            pltpu.VMEM((nchunk, nrow, npg * prow), F32),
            pltpu.VMEM((nchunk, nrow, npg * prow), BF16),
            pltpu.VMEM((nblk * SUBLANES, HEAD_DIM), F32),
            pltpu.VMEM((nrow, HEAD_DIM), F32),
            pltpu.VMEM((nrow, 1), F32),
        ],
    )
    o = pl.pallas_call(
        functools.partial(_attn_sample_kernel, npg=npg, dec_seq=dec_seq),
        grid_spec=grid_spec,
        out_shape=jax.ShapeDtypeStruct((batch, nrow, HEAD_DIM), F32),
        compiler_params=_cparams(("parallel", "arbitrary", "arbitrary")),
        name="attn_sample",
    )(page_table, qd, k_new.reshape(batch, new_rows, HEAD_DIM),
      v_new.reshape(batch, new_rows, HEAD_DIM), *([cache_k] * npg), *([cache_v] * npg))
    o = o.reshape(batch, N_HEADS, dec_seq, HEAD_DIM).transpose(0, 2, 1, 3)
    return o.reshape(batch * dec_seq, N_HEADS * HEAD_DIM)


def _oproj_kernel(o_ref, h_ref, wo_ref, out_ref):
    out_ref[...] = h_ref[...] + _dot(o_ref[...].astype(BF16), wo_ref[...])


def _oproj(o, h, wo):
    return pl.pallas_call(
        _oproj_kernel,
        out_shape=jax.ShapeDtypeStruct(h.shape, F32),
        compiler_params=pltpu.CompilerParams(vmem_limit_bytes=VMEM_LIMIT_BYTES),
        name="oproj",
    )(o, h, wo)


def _moe_kernel(h_ref, n2_ref, wr_ref, br_ref, wg_ref, wu_ref, wd_ref, out_ref,
                xn_scr, gates_scr, acc_scr):
    g = pl.program_id(1)
    ngroups = pl.num_programs(1)
    tm = h_ref.shape[0]
    lane = lax.broadcasted_iota(jnp.int32, (tm, LANES), 1)
    first_expert_lane = N_EXPERT_GROUPS

    @pl.when(g == 0)
    def _():
        x = h_ref[...]
        xn = _rms(x, n2_ref[...])
        xn_scr[...] = xn.astype(BF16)
        acc_scr[...] = x
        logits = _dot3(xn, wr_ref[...]) + br_ref[...]
        is_group = lane < N_EXPERT_GROUPS
        gl = jnp.where(is_group, logits, NEG_INF)
        ge = jnp.where(is_group, jnp.exp(gl - jnp.max(gl, axis=-1, keepdims=True)), 0.0)
        gprob = ge / jnp.sum(ge, axis=-1, keepdims=True)
        g_w = jnp.max(gprob, axis=-1, keepdims=True)
        g_idx = jnp.min(jnp.where(is_group & (gprob == g_w), lane, LANES), axis=-1, keepdims=True)
        lo = first_expert_lane + g_idx * EXPERTS_PER_GROUP
        in_group = (lane >= lo) & (lane < lo + EXPERTS_PER_GROUP)
        el = jnp.where(in_group, logits, NEG_INF)
        ee = jnp.where(in_group, jnp.exp(el - jnp.max(el, axis=-1, keepdims=True)), 0.0)
        eprob = jnp.where(in_group, ee / jnp.sum(ee, axis=-1, keepdims=True), -1.0)
        p1 = jnp.max(eprob, axis=-1, keepdims=True)
        i1 = jnp.min(jnp.where(eprob == p1, lane, LANES), axis=-1, keepdims=True)
        rest = jnp.where(lane == i1, -1.0, eprob)
        p2 = jnp.max(rest, axis=-1, keepdims=True)
        i2 = jnp.min(jnp.where(rest == p2, lane, LANES), axis=-1, keepdims=True)
        denom = p1 + p2
        w1 = p1 / denom * g_w
        w2 = p2 / denom * g_w
        gates_scr[...] = jnp.where(lane == i1, w1, 0.0) + jnp.where(lane == i2, w2, 0.0)

    xb = xn_scr[...]
    gates = gates_scr[...]
    acc = acc_scr[...]
    for e in range(EXPERTS_PER_GROUP):
        a = _dot(xb, wg_ref[e])
        b = _dot(xb, wu_ref[e])
        tgt = first_expert_lane + g * EXPERTS_PER_GROUP + e
        gate = jnp.sum(jnp.where(lane == tgt, gates, 0.0), axis=-1, keepdims=True)
        hid = _silu(a) * b * gate
        acc = acc + _dot(hid.astype(BF16), wd_ref[e])
    acc_scr[...] = acc

    @pl.when(g == ngroups - 1)
    def _():
        out_ref[...] = acc_scr[...]


def _moe(h, n2, wr, br, wg, wu, wd, *, tm):
    n, d = h.shape
    de = wg.shape[2]
    epg = EXPERTS_PER_GROUP
    const = lambda i, g: (0, 0)
    return pl.pallas_call(
        _moe_kernel,
        grid=(n // tm, N_EXPERT_GROUPS),
        in_specs=[
            pl.BlockSpec((tm, d), lambda i, g: (i, 0)),
            pl.BlockSpec((1, d), const),
            pl.BlockSpec((d, LANES), const),
            pl.BlockSpec((1, LANES), const),
            pl.BlockSpec((epg, d, de), lambda i, g: (g, 0, 0)),
            pl.BlockSpec((epg, d, de), lambda i, g: (g, 0, 0)),
            pl.BlockSpec((epg, de, d), lambda i, g: (g, 0, 0)),
        ],
        out_specs=pl.BlockSpec((tm, d), lambda i, g: (i, 0)),
        out_shape=jax.ShapeDtypeStruct((n, d), F32),
        scratch_shapes=[
            pltpu.VMEM((tm, d), BF16),
            pltpu.VMEM((tm, LANES), F32),
            pltpu.VMEM((tm, d), F32),
        ],
        compiler_params=_cparams(("parallel", "arbitrary")),
        name="moe",
    )(h, n2, wr, br, wg, wu, wd)


def kernel(x_prompt, x_sample, cache_k, cache_v, page_table, norm1, norm2, a_w_in, a_g_v, a_w_s, a_b_s, a_w_out, b_w_qkv, b_q_norm, b_k_norm, b_w_o, moe_w_group, moe_b_group, moe_w_router, moe_b_router, moe_w_gate, moe_w_up, moe_w_down):
    bp, tp, d = x_prompt.shape
    bs, ts, _ = x_sample.shape
    depth = norm1.shape[0]
    n_attn, n_pool, page = cache_k.shape[0], cache_k.shape[1], cache_k.shape[2]
    past_len = page_table.shape[1] * page
    assert CHUNK % ts == 0 and (bs * ts) % CHUNK == 0 and tp % MOBA_BLOCK == 0

    hp = x_prompt.reshape(bp * tp, d)
    hs = x_sample.reshape(bs * ts, d)
    ck = cache_k.reshape(n_attn * n_pool, page * N_KV_HEADS, HEAD_DIM)
    cv = cache_v.reshape(n_attn * n_pool, page * N_KV_HEADS, HEAD_DIM)
    tab_p = _rope_tables(jnp.arange(tp, dtype=jnp.int32))
    pos_s = past_len + jnp.arange(ts, dtype=jnp.int32)
    tab_s = tuple(jnp.tile(t, (bs, 1)) for t in _rope_tables(pos_s))

    pad = LANES - N_EXPERT_GROUPS - N_EXPERTS
    wr_all = jnp.concatenate(
        [moe_w_group, moe_w_router, jnp.zeros((depth, d, pad), F32)], axis=2)
    br_all = jnp.concatenate(
        [moe_b_group, moe_b_router, jnp.zeros((depth, pad), F32)], axis=1)[:, None, :]

    new_k_p, new_v_p, new_k_s, new_v_s, new_chunk_v = [], [], [], [], []
    reps = CHUNK // ts
    for i in range(depth):
        j = i // 2
        n1 = norm1[i][None, :]
        if i % 2 == 0:
            w_in = a_w_in[j].astype(BF16)
            w_out = a_w_out[j].astype(BF16)
            g_v = a_g_v[j][None, :]
            hp = _mixer_a(hp, n1, w_in, g_v, a_w_s[j], a_b_s[j][:, :, None], w_out,
                          tm=512, seq_len=CHUNK, emit_v=False)
            wmix_s = jnp.tile(a_w_s[j][:, :ts, :ts], (1, reps, reps))
            bs_s = jnp.tile(a_b_s[j][:, :ts], (1, reps))[:, :, None]
            hs, v_rows = _mixer_a(hs, n1, w_in, g_v, wmix_s, bs_s, w_out,
                                  tm=CHUNK, seq_len=ts, emit_v=True)
            new_chunk_v.append(v_rows.reshape(bs, ts, -1))
        else:
            w_qkv = b_w_qkv[j].astype(BF16)
            w_o = b_w_o[j].astype(BF16)
            qn = b_q_norm[j][None, :]
            kn = b_k_norm[j][None, :]
            qp, kp, vp = _qkv(hp, n1, w_qkv, qn, kn, tab_p, tm=512)
            qs, ks_new, vs_new = _qkv(hs, n1, w_qkv, qn, kn, tab_s, tm=bs * ts)
            hp = _attn_prompt(qp, kp, vp, hp, w_o, batch=bp, seq=tp)
            o_s = _attn_sample(qs, ks_new, vs_new, ck, cv, page_table, j, n_pool=n_pool,
                               batch=bs, dec_seq=ts)
            hs = _oproj(o_s, hs, w_o)
            new_k_p.append(kp.reshape(bp, tp, N_KV_HEADS, HEAD_DIM))
            new_v_p.append(vp.reshape(bp, tp, N_KV_HEADS, HEAD_DIM))
            new_k_s.append(ks_new.reshape(bs, ts, N_KV_HEADS, HEAD_DIM))
            new_v_s.append(vs_new.reshape(bs, ts, N_KV_HEADS, HEAD_DIM))
        n2 = norm2[i][None, :]
        wg = moe_w_gate[i].astype(BF16)
        wu = moe_w_up[i].astype(BF16)
        wd = moe_w_down[i].astype(BF16)
        hp = _moe(hp, n2, wr_all[i], br_all[i], wg, wu, wd, tm=min(1024, bp * tp))
        hs = _moe(hs, n2, wr_all[i], br_all[i], wg, wu, wd, tm=bs * ts)
    return (hp.reshape(bp, tp, d), hs.reshape(bs, ts, d),
            jnp.stack(new_k_p), jnp.stack(new_v_p), jnp.stack(new_k_s), jnp.stack(new_v_s),
            jnp.stack(new_chunk_v))
```

```python
import functools

import jax
import jax.numpy as jnp
from jax import lax
from jax.experimental import pallas as pl
from jax.experimental.pallas import tpu as pltpu

F32 = jnp.float32
BF16 = jnp.bfloat16

CHUNK = 128
CHUNK_GROUPS = 4
HEAD_DIM = 128
N_HEADS = 8
N_KV_HEADS = 4
KV_GROUP = N_HEADS // N_KV_HEADS
ROT_DIM = HEAD_DIM // 4
ROPE_THETA = 500000.0
MOBA_BLOCK = 256
MOBA_TOP_K = 3
N_EXPERT_GROUPS = 4
EXPERTS_PER_GROUP = 4
N_EXPERTS = N_EXPERT_GROUPS * EXPERTS_PER_GROUP
NORM_EPS = 1e-6
NEG_INF = -1e30
LOG2E = 1.4426950408889634
SPAN_BLOCKS = 2

SUBLANES = 8
LANES = 128
VMEM_LIMIT_BYTES = 56 * 1024 * 1024

_NT = (((1,), (1,)), ((), ()))


def _dot(a, b):
    return jnp.dot(a, b, preferred_element_type=F32)


def _dot_nt(a, b):
    return lax.dot_general(a, b, _NT, preferred_element_type=F32)


def _split_bf16(a):
    hi = a.astype(BF16)
    lo = (a - hi.astype(F32)).astype(BF16)
    return hi, lo


def _dot3(a, b, nt=False):
    d = _dot_nt if nt else _dot
    ah, al = _split_bf16(a)
    bh, bl = _split_bf16(b)
    return d(ah, bh) + (d(ah, bl) + d(al, bh))


def _mm(x, w, precise):
    if precise:
        return _dot3(x, w)
    return _dot(x.astype(BF16), w.astype(BF16))


def _rms(x, g):
    ms = jnp.mean(x * x, axis=-1, keepdims=True)
    return x * lax.rsqrt(ms + NORM_EPS) * g


def _gelu_tanh(x):
    c = 0.7978845608028654
    return 0.5 * x * (1.0 + jnp.tanh(c * (x + 0.044715 * (x * x * x))))


def _silu(x):
    return x * (1.0 / (1.0 + jnp.exp(-x)))


def _cparams(sem):
    return pltpu.CompilerParams(dimension_semantics=sem, vmem_limit_bytes=VMEM_LIMIT_BYTES)


def _mixer_a_kernel(h_ref, n1_ref, win_ref, gv_ref, wmix_ref, bs_ref, wout_ref, *rest,
                    seq_len, emit_v, precise):
    if emit_v:
        out_ref, v_ref, us_scr = rest
    else:
        out_ref, us_scr = rest
    tm = h_ref.shape[0]
    width = gv_ref.shape[1]
    gdim = width // CHUNK_GROUPS
    x = h_ref[...]
    xn = _rms(x, n1_ref[...])
    z = _gelu_tanh(_mm(xn, win_ref[...], precise))
    u = z[:, :width]
    vn = _rms(z[:, width:], gv_ref[...])
    if emit_v:
        v_ref[...] = vn
    row = lax.broadcasted_iota(jnp.int32, (CHUNK, CHUNK), 0)
    col = lax.broadcasted_iota(jnp.int32, (CHUNK, CHUNK), 1)
    mask = col <= row
    if seq_len < CHUNK:
        mask = mask & ((row // seq_len) == (col // seq_len))
    if not precise:
        vn = vn.astype(BF16)
    for g in range(CHUNK_GROUPS):
        wm = jnp.where(mask, wmix_ref[g], 0.0)
        bias = bs_ref[g]
        for c in range(tm // CHUNK):
            rows = slice(c * CHUNK, (c + 1) * CHUNK)
            cols = slice(g * gdim, (g + 1) * gdim)
            s = _mm(wm, vn[rows, cols], precise) + bias
            us_scr[rows, cols] = (u[rows, cols] * s).astype(us_scr.dtype)
    out_ref[...] = x + _mm(us_scr[...], wout_ref[...], precise)


def _mixer_a(h, n1, w_in, g_v, wmix, bs, w_out, *, tm, seq_len, emit_v, precise):
    n, d = h.shape
    width = g_v.shape[1]
    const = lambda i: (0, 0)
    out_shape = [jax.ShapeDtypeStruct((n, d), F32)]
    out_specs = [pl.BlockSpec((tm, d), lambda i: (i, 0))]
    if emit_v:
        out_shape.append(jax.ShapeDtypeStruct((n, width), F32))
        out_specs.append(pl.BlockSpec((tm, width), lambda i: (i, 0)))
    res = pl.pallas_call(
        functools.partial(_mixer_a_kernel, seq_len=seq_len, emit_v=emit_v, precise=precise),
        grid=(n // tm,),
        in_specs=[
            pl.BlockSpec((tm, d), lambda i: (i, 0)),
            pl.BlockSpec((1, d), const),
            pl.BlockSpec((d, 2 * width), const),
            pl.BlockSpec((1, width), const),
            pl.BlockSpec((CHUNK_GROUPS, CHUNK, CHUNK), lambda i: (0, 0, 0)),
            pl.BlockSpec((CHUNK_GROUPS, CHUNK, 1), lambda i: (0, 0, 0)),
            pl.BlockSpec((width, d), const),
        ],
        out_specs=out_specs,
        out_shape=out_shape,
        scratch_shapes=[pltpu.VMEM((tm, width), F32 if precise else BF16)],
        compiler_params=_cparams(("parallel",)),
        name="mixer_a_sample" if emit_v else "mixer_a",
    )(h, n1, w_in, g_v, wmix, bs, w_out)
    return res if emit_v else res[0]


def _qkv_kernel(h_ref, n1_ref, w_ref, qn_ref, kn_ref, cos_ref, sa_ref, sb_ref,
                q_ref, k_ref, v_ref, *, precise):
    xn = _rms(h_ref[...], n1_ref[...])
    qkv = _mm(xn, w_ref[...], precise)
    cos = cos_ref[...]
    sa = sa_ref[...]
    sb = sb_ref[...]
    nq = q_ref.shape[1]
    nk = k_ref.shape[1]

    def head(t, g):
        t = _rms(t, g)
        return (t * cos + pltpu.roll(t, HEAD_DIM - ROT_DIM // 2, 1) * sa
                + pltpu.roll(t, ROT_DIM // 2, 1) * sb)

    for hd in range(nq // HEAD_DIM):
        cols = slice(hd * HEAD_DIM, (hd + 1) * HEAD_DIM)
        q_ref[:, cols] = head(qkv[:, cols], qn_ref[...])
    for hd in range(nk // HEAD_DIM):
        cols = slice(hd * HEAD_DIM, (hd + 1) * HEAD_DIM)
        k_ref[:, cols] = head(qkv[:, nq + hd * HEAD_DIM: nq + (hd + 1) * HEAD_DIM], kn_ref[...])
    v_ref[...] = qkv[:, nq + nk:]


def _rope_tables(pos):
    half = ROT_DIM // 2
    inv_freq = jnp.power(ROPE_THETA, -jnp.arange(half, dtype=F32) * 2.0 / ROT_DIM)
    ang = pos.astype(F32)[:, None] * inv_freq[None, :]
    cos = jnp.cos(ang)
    sin = jnp.sin(ang)
    t = pos.shape[0]
    ones = jnp.ones((t, HEAD_DIM - ROT_DIM), F32)
    zeros = jnp.zeros((t, HEAD_DIM - ROT_DIM), F32)
    z16 = jnp.zeros((t, half), F32)
    cos_t = jnp.concatenate([cos, cos, ones], axis=1)
    sa_t = jnp.concatenate([-sin, z16, zeros], axis=1)
    sb_t = jnp.concatenate([z16, sin, zeros], axis=1)
    return cos_t, sa_t, sb_t


def _qkv(h, n1, w, qn, kn, tables, *, tm, precise):
    n, d = h.shape
    nq = N_HEADS * HEAD_DIM
    nk = N_KV_HEADS * HEAD_DIM
    t = tables[0].shape[0]
    tiles_per_seq = t // tm
    const = lambda i: (0, 0)
    tab = pl.BlockSpec((tm, HEAD_DIM), lambda i: (i % tiles_per_seq, 0))
    return pl.pallas_call(
        functools.partial(_qkv_kernel, precise=precise),
        grid=(n // tm,),
        in_specs=[
            pl.BlockSpec((tm, d), lambda i: (i, 0)),
            pl.BlockSpec((1, d), const),
            pl.BlockSpec((d, nq + 2 * nk), const),
            pl.BlockSpec((1, HEAD_DIM), const),
            pl.BlockSpec((1, HEAD_DIM), const),
            tab, tab, tab,
        ],
        out_specs=[
            pl.BlockSpec((tm, nq), lambda i: (i, 0)),
            pl.BlockSpec((tm, nk), lambda i: (i, 0)),
            pl.BlockSpec((tm, nk), lambda i: (i, 0)),
        ],
        out_shape=[
            jax.ShapeDtypeStruct((n, nq), F32),
            jax.ShapeDtypeStruct((n, nk), F32),
            jax.ShapeDtypeStruct((n, nk), F32),
        ],
        compiler_params=_cparams(("parallel",)),
        name="qkv_sample" if precise else "qkv",
    )(h, n1, w, qn, kn, *tables)


def _attn_prompt_kernel(q_ref, k_ref, v_ref, h_ref, wo_ref, out_ref,
                        kbf, vtb, kmean, s_scr, p_scr, o_scr):
    blk = MOBA_BLOCK
    nblk = k_ref.shape[0] // blk
    own = pl.program_id(1)
    scale = HEAD_DIM ** -0.5

    @pl.when(own == 0)
    def _():
        kbf[...] = k_ref[...].astype(BF16)
        for n in range(nblk):
            rows = slice(n * blk, (n + 1) * blk)
            kmean[n:n + 1, :] = jnp.mean(k_ref[rows, :], axis=0, keepdims=True)
            vtb[:, rows] = v_ref[rows, :].T.astype(BF16)

    cols = KV_GROUP * blk
    nidx = lax.broadcasted_iota(jnp.int32, (nblk, cols), 0)
    kq_diff = (lax.broadcasted_iota(jnp.int32, (blk, cols), 0)
               - lax.broadcasted_iota(jnp.int32, (blk, cols), 1) % blk)

    for kh in range(N_KV_HEADS):
        kcols = slice(kh * HEAD_DIM, (kh + 1) * HEAD_DIM)
        q2 = jnp.concatenate(
            [q_ref[:, (KV_GROUP * kh + i) * HEAD_DIM:(KV_GROUP * kh + i + 1) * HEAD_DIM]
             for i in range(KV_GROUP)], axis=0)
        q2s = (q2 * (scale * LOG2E)).astype(BF16)

        g_t = _dot3(kmean[:, kcols], q2, nt=True)
        rank = jnp.zeros_like(g_t)
        for m in range(nblk - 1):
            gm = g_t[m:m + 1, :]
            beats = (gm > g_t) | ((gm == g_t) & (nidx > m))
            inc = jnp.where(m < own, 1.0, 0.0)
            rank = rank + jnp.where(beats, inc, 0.0)
        sel_t = jnp.where(nidx == own, 1.0,
                          jnp.where((nidx < own) & (rank < float(MOBA_TOP_K)), 1.0, 0.0))

        def attend(nb, kh=kh, kcols=kcols, q2s=q2s, sel_t=sel_t):
            m = jnp.full((1, cols), NEG_INF, F32)
            for n in range(nb):
                rows = slice(n * blk, (n + 1) * blk)
                s = _dot_nt(kbf[rows, kcols], q2s)
                s = jnp.where(sel_t[n:n + 1, :] > 0.5, s, NEG_INF)
                if n >= nb - SPAN_BLOCKS:
                    s = jnp.where(kq_diff <= (own - n) * blk, s, NEG_INF)
                s_scr[rows, :] = s
                m = jnp.maximum(m, jnp.max(s, axis=0, keepdims=True))
            l = jnp.zeros((1, cols), F32)
            for n in range(nb):
                rows = slice(n * blk, (n + 1) * blk)
                p = jnp.exp2(s_scr[rows, :] - m)
                l = l + jnp.sum(p, axis=0, keepdims=True)
                p_scr[rows, :] = p.astype(BF16)
            acc = _dot(vtb[kcols, :nb * blk], p_scr[:nb * blk, :])
            o = (acc / l).T.astype(BF16)
            for i in range(KV_GROUP):
                hd = KV_GROUP * kh + i
                o_scr[:, hd * HEAD_DIM:(hd + 1) * HEAD_DIM] = o[i * blk:(i + 1) * blk]

        for nb in range(SPAN_BLOCKS, nblk + SPAN_BLOCKS, SPAN_BLOCKS):
            pl.when((own >= nb - SPAN_BLOCKS) & (own < nb))(functools.partial(attend, min(nb, nblk)))

    out_ref[...] = h_ref[...] + _dot(o_scr[...], wo_ref[...])


def _attn_prompt(q, k, v, h, wo, *, batch, seq):
    blk = MOBA_BLOCK
    nqb = seq // blk
    d = h.shape[1]
    nq = q.shape[1]
    nk = k.shape[1]
    cols = KV_GROUP * blk
    return pl.pallas_call(
        _attn_prompt_kernel,
        grid=(batch, nqb),
        in_specs=[
            pl.BlockSpec((blk, nq), lambda b, i: (b * nqb + i, 0)),
            pl.BlockSpec((seq, nk), lambda b, i: (b, 0)),
            pl.BlockSpec((seq, nk), lambda b, i: (b, 0)),
            pl.BlockSpec((blk, d), lambda b, i: (b * nqb + i, 0)),
            pl.BlockSpec((nq, d), lambda b, i: (0, 0)),
        ],
        out_specs=pl.BlockSpec((blk, d), lambda b, i: (b * nqb + i, 0)),
        out_shape=jax.ShapeDtypeStruct(h.shape, F32),
        scratch_shapes=[
            pltpu.VMEM((seq, nk), BF16),
            pltpu.VMEM((nk, seq), BF16),
            pltpu.VMEM((nqb, nk), F32),
            pltpu.VMEM((seq, cols), F32),
            pltpu.VMEM((seq, cols), BF16),
            pltpu.VMEM((blk, nq), BF16),
        ],
        compiler_params=_cparams(("parallel", "arbitrary")),
        name="attn_prompt",
    )(q, k, v, h, wo)


def _attn_sample_kernel(pt_ref, qd_ref, knew_ref, vnew_ref, ck_hbm, cv_hbm, out_ref,
                        kbuf, vbuf, s_scr, p_scr, kmean_scr, ksem, vsem,
                        *, base, npg, dec_seq):
    b = pl.program_id(0)
    nbatch = pl.num_programs(0)
    n_pages, prow, _ = kbuf.shape
    nchunk = n_pages // npg
    ccols = npg * prow
    bcols = MOBA_BLOCK * N_KV_HEADS
    pages_per_blk = bcols // prow
    blk_per_chunk = npg // pages_per_blk
    nrow = qd_ref.shape[0]
    nl = kmean_scr.shape[0]
    nblk = nl // SUBLANES
    scale = HEAD_DIM ** -0.5

    def page_copy(hbm, buf, sem, bb, pg):
        return pltpu.make_async_copy(hbm.at[base + pt_ref[bb, pg]], buf.at[pg], sem.at[pg // npg])

    def fetch(hbm, buf, sem, bb):
        for pg in range(n_pages):
            page_copy(hbm, buf, sem, bb, pg).start()

    def wait_chunk(hbm, buf, sem, c):
        for pg in range(c * npg, (c + 1) * npg):
            page_copy(hbm, buf, sem, b, pg).wait()

    @pl.when(b == 0)
    def _():
        fetch(ck_hbm, kbuf, ksem, 0)
        fetch(cv_hbm, vbuf, vsem, 0)

    qd = qd_ref[...]
    q_hi, q_lo = _split_bf16(qd)
    q_hl = jnp.concatenate([q_hi, q_lo], axis=0)

    for c in range(nchunk):
        wait_chunk(ck_hbm, kbuf, ksem, c)
        kf = [kbuf[c * npg + i] for i in range(npg)]
        for i in range(blk_per_chunk):
            tot = kf[pages_per_blk * i]
            for j in range(1, pages_per_blk):
                tot = tot + kf[pages_per_blk * i + j]
            part = jnp.sum(tot.reshape(prow // SUBLANES, SUBLANES, HEAD_DIM), axis=0)
            part = (part + pltpu.roll(part, N_KV_HEADS, 0)) * (1.0 / MOBA_BLOCK)
            row0 = (c * blk_per_chunk + i) * SUBLANES
            kmean_scr[row0:row0 + SUBLANES, :] = part
        k_hi, k_lo = _split_bf16(jnp.concatenate(kf, axis=0))
        s2 = _dot_nt(q_hl, k_hi)
        s_scr[:, c * ccols:(c + 1) * ccols] = (s2[:nrow] + s2[nrow:]) + _dot_nt(q_hi, k_lo)

    @pl.when(b + 1 < nbatch)
    def _():
        fetch(ck_hbm, kbuf, ksem, b + 1)

    g = _dot3(qd, kmean_scr[...], nt=True)
    lane = lax.broadcasted_iota(jnp.int32, (nrow, nl), 1)
    kv_r = (lax.broadcasted_iota(jnp.int32, (nrow, nl), 0) // dec_seq) // KV_GROUP
    rowsel = [jnp.where(kv_r == kh, 1.0, 0.0) for kh in range(N_KV_HEADS)]
    rank = jnp.zeros_like(g)
    for m in range(nblk):
        for kh in range(N_KV_HEADS):
            pos = m * SUBLANES + kh
            col = g[:, pos:pos + 1]
            beats = (col > g) | ((col == g) & (lane > pos))
            rank = rank + jnp.where(beats, rowsel[kh], 0.0)
    sel = jnp.where((lane % SUBLANES) == kv_r,
                    jnp.where(rank < float(MOBA_TOP_K), 1.0, 0.0), 0.0)
    blkid = lane // SUBLANES

    kv_c = (lax.broadcasted_iota(jnp.int32, (nrow, 1), 0) // dec_seq) // KV_GROUP
    t_row = lax.broadcasted_iota(jnp.int32, (nrow, 1), 0) % dec_seq

    def rows_for(x, t):
        out = jnp.zeros((nrow, HEAD_DIM), F32)
        for kh in range(N_KV_HEADS):
            r = t * N_KV_HEADS + kh
            out = out + jnp.where(kv_c == kh, x[r:r + 1, :], 0.0)
        return out

    knew = knew_ref[...]
    vnew = vnew_ref[...]
    s_own = [jnp.sum(qd * rows_for(knew, t), axis=-1, keepdims=True) * scale
             for t in range(dec_seq)]
    ok_own = [t_row >= t for t in range(dec_seq)]

    kh_match = (lax.broadcasted_iota(jnp.int32, (nrow, bcols), 1) % N_KV_HEADS) == (
        (lax.broadcasted_iota(jnp.int32, (nrow, bcols), 0) // dec_seq) // KV_GROUP)
    m = jnp.full((nrow, 1), NEG_INF, F32)
    for t in range(dec_seq):
        m = jnp.maximum(m, jnp.where(ok_own[t], s_own[t], NEG_INF))
    for n in range(nblk):
        cs = slice(n * bcols, (n + 1) * bcols)
        pick = jnp.max(jnp.where(blkid == n, sel, 0.0), axis=1, keepdims=True)
        sc = jnp.where(pick > 0.5, s_scr[:, cs] * scale, NEG_INF)
        sc = jnp.where(kh_match, sc, NEG_INF)
        s_scr[:, cs] = sc
        m = jnp.maximum(m, jnp.max(sc, axis=-1, keepdims=True))
    l = jnp.zeros((nrow, 1), F32)
    for n in range(nblk):
        cs = slice(n * bcols, (n + 1) * bcols)
        p = jnp.exp(s_scr[:, cs] - m)
        l = l + jnp.sum(p, axis=-1, keepdims=True)
        p_hi, p_lo = _split_bf16(p)
        p_scr[:nrow, cs] = p_hi
        p_scr[nrow:, cs] = p_lo
    acc = jnp.zeros((nrow, HEAD_DIM), F32)
    for t in range(dec_seq):
        p_t = jnp.where(ok_own[t], jnp.exp(s_own[t] - m), 0.0)
        l = l + p_t
        acc = acc + p_t * rows_for(vnew, t)

    for c in range(nchunk):
        wait_chunk(cv_hbm, vbuf, vsem, c)
        vb = jnp.concatenate([vbuf[c * npg + i].astype(BF16) for i in range(npg)], axis=0)
        pv = _dot(p_scr[:, c * ccols:(c + 1) * ccols], vb)
        acc = acc + (pv[:nrow] + pv[nrow:])

    @pl.when(b + 1 < nbatch)
    def _():
        fetch(cv_hbm, vbuf, vsem, b + 1)

    out_ref[...] = acc / l


def _attn_sample(q, k_new, v_new, cache_k, cache_v, page_table, layer, *, n_pool, batch, dec_seq,
                 npg=8):
    n_pages = page_table.shape[1]
    prow = cache_k.shape[1]
    page = prow // N_KV_HEADS
    assert (n_pages * page) % MOBA_BLOCK == 0 and dec_seq <= MOBA_BLOCK
    assert n_pages % npg == 0 and MOBA_BLOCK % page == 0 and npg % (MOBA_BLOCK // page) == 0
    assert SUBLANES == 2 * N_KV_HEADS and prow % SUBLANES == 0
    nchunk = n_pages // npg
    nblk = n_pages * page // MOBA_BLOCK
    nrow = N_HEADS * dec_seq
    qd = q.reshape(batch, dec_seq, N_HEADS, HEAD_DIM).transpose(0, 2, 1, 3)
    qd = qd.reshape(batch, nrow, HEAD_DIM)
    per_b = lambda b, pt: (b, 0, 0)
    new_rows = dec_seq * N_KV_HEADS
    grid_spec = pltpu.PrefetchScalarGridSpec(
        num_scalar_prefetch=1,
        grid=(batch,),
        in_specs=[
            pl.BlockSpec((None, nrow, HEAD_DIM), per_b),
            pl.BlockSpec((None, new_rows, HEAD_DIM), per_b),
            pl.BlockSpec((None, new_rows, HEAD_DIM), per_b),
            pl.BlockSpec(memory_space=pl.ANY),
            pl.BlockSpec(memory_space=pl.ANY),
        ],
        out_specs=pl.BlockSpec((None, nrow, HEAD_DIM), per_b),
        scratch_shapes=[
            pltpu.VMEM((n_pages, prow, HEAD_DIM), F32),
            pltpu.VMEM((n_pages, prow, HEAD_DIM), F32),
            pltpu.VMEM((nrow, n_pages * prow), F32),
            pltpu.VMEM((2 * nrow, n_pages * prow), BF16),
            pltpu.VMEM((nblk * SUBLANES, HEAD_DIM), F32),
            pltpu.SemaphoreType.DMA((nchunk,)),
            pltpu.SemaphoreType.DMA((nchunk,)),
        ],
    )
    o = pl.pallas_call(
        functools.partial(_attn_sample_kernel, base=layer * n_pool, npg=npg, dec_seq=dec_seq),
        grid_spec=grid_spec,
        out_shape=jax.ShapeDtypeStruct((batch, nrow, HEAD_DIM), F32),
        compiler_params=_cparams(("arbitrary",)),
        name="attn_sample",
    )(page_table, qd, k_new.reshape(batch, new_rows, HEAD_DIM),
      v_new.reshape(batch, new_rows, HEAD_DIM), cache_k, cache_v)
    o = o.reshape(batch, N_HEADS, dec_seq, HEAD_DIM).transpose(0, 2, 1, 3)
    return o.reshape(batch * dec_seq, N_HEADS * HEAD_DIM)


def _oproj_kernel(o_ref, h_ref, wo_ref, out_ref):
    out_ref[...] = h_ref[...] + _dot3(o_ref[...], wo_ref[...])


def _oproj(o, h, wo):
    return pl.pallas_call(
        _oproj_kernel,
        out_shape=jax.ShapeDtypeStruct(h.shape, F32),
        compiler_params=pltpu.CompilerParams(vmem_limit_bytes=VMEM_LIMIT_BYTES),
        name="oproj_sample",
    )(o, h, wo)


def _moe_kernel(h_ref, n2_ref, wr_ref, br_ref, wg_ref, wu_ref, wd_ref, out_ref,
                xn_scr, gates_scr, acc_scr, *, precise):
    g = pl.program_id(1)
    ngroups = pl.num_programs(1)
    tm = h_ref.shape[0]
    lane = lax.broadcasted_iota(jnp.int32, (tm, LANES), 1)
    first_expert_lane = N_EXPERT_GROUPS

    @pl.when(g == 0)
    def _():
        x = h_ref[...]
        xn = _rms(x, n2_ref[...])
        xn_scr[...] = xn.astype(xn_scr.dtype)
        acc_scr[...] = x
        logits = _dot3(xn, wr_ref[...]) + br_ref[...]
        is_group = lane < N_EXPERT_GROUPS
        gl = jnp.where(is_group, logits, NEG_INF)
        ge = jnp.where(is_group, jnp.exp(gl - jnp.max(gl, axis=-1, keepdims=True)), 0.0)
        gprob = ge / jnp.sum(ge, axis=-1, keepdims=True)
        g_w = jnp.max(gprob, axis=-1, keepdims=True)
        g_idx = jnp.min(jnp.where(is_group & (gprob == g_w), lane, LANES), axis=-1, keepdims=True)
        lo = first_expert_lane + g_idx * EXPERTS_PER_GROUP
        in_group = (lane >= lo) & (lane < lo + EXPERTS_PER_GROUP)
        el = jnp.where(in_group, logits, NEG_INF)
        ee = jnp.where(in_group, jnp.exp(el - jnp.max(el, axis=-1, keepdims=True)), 0.0)
        eprob = jnp.where(in_group, ee / jnp.sum(ee, axis=-1, keepdims=True), -1.0)
        p1 = jnp.max(eprob, axis=-1, keepdims=True)
        i1 = jnp.min(jnp.where(eprob == p1, lane, LANES), axis=-1, keepdims=True)
        rest = jnp.where(lane == i1, -1.0, eprob)
        p2 = jnp.max(rest, axis=-1, keepdims=True)
        i2 = jnp.min(jnp.where(rest == p2, lane, LANES), axis=-1, keepdims=True)
        denom = p1 + p2
        w1 = p1 / denom * g_w
        w2 = p2 / denom * g_w
        gates_scr[...] = jnp.where(lane == i1, w1, 0.0) + jnp.where(lane == i2, w2, 0.0)

    xb = xn_scr[...]
    gates = gates_scr[...]
    acc = acc_scr[...]
    for e in range(EXPERTS_PER_GROUP):
        a = _mm(xb, wg_ref[e], precise)
        b = _mm(xb, wu_ref[e], precise)
        tgt = first_expert_lane + g * EXPERTS_PER_GROUP + e
        gate = jnp.sum(jnp.where(lane == tgt, gates, 0.0), axis=-1, keepdims=True)
        hid = _silu(a) * b * gate
        acc = acc + _mm(hid, wd_ref[e], precise)
    acc_scr[...] = acc

    @pl.when(g == ngroups - 1)
    def _():
        out_ref[...] = acc_scr[...]


def _moe(h, n2, wr, br, wg, wu, wd, *, tm, precise):
    n, d = h.shape
    de = wg.shape[2]
    epg = EXPERTS_PER_GROUP
    const = lambda i, g: (0, 0)
    return pl.pallas_call(
        functools.partial(_moe_kernel, precise=precise),
        grid=(n // tm, N_EXPERT_GROUPS),
        in_specs=[
            pl.BlockSpec((tm, d), lambda i, g: (i, 0)),
            pl.BlockSpec((1, d), const),
            pl.BlockSpec((d, LANES), const),
            pl.BlockSpec((1, LANES), const),
            pl.BlockSpec((epg, d, de), lambda i, g: (g, 0, 0)),
            pl.BlockSpec((epg, d, de), lambda i, g: (g, 0, 0)),
            pl.BlockSpec((epg, de, d), lambda i, g: (g, 0, 0)),
        ],
        out_specs=pl.BlockSpec((tm, d), lambda i, g: (i, 0)),
        out_shape=jax.ShapeDtypeStruct((n, d), F32),
        scratch_shapes=[
            pltpu.VMEM((tm, d), F32 if precise else BF16),
            pltpu.VMEM((tm, LANES), F32),
            pltpu.VMEM((tm, d), F32),
        ],
        compiler_params=_cparams(("parallel", "arbitrary")),
        name="moe_sample" if precise else "moe",
    )(h, n2, wr, br, wg, wu, wd)


def kernel(x_prompt, x_sample, cache_k, cache_v, page_table, norm1, norm2, a_w_in, a_g_v, a_w_s, a_b_s, a_w_out, b_w_qkv, b_q_norm, b_k_norm, b_w_o, moe_w_group, moe_b_group, moe_w_router, moe_b_router, moe_w_gate, moe_w_up, moe_w_down):
    bp, tp, d = x_prompt.shape
    bs, ts, _ = x_sample.shape
    depth = norm1.shape[0]
    n_attn, n_pool, page = cache_k.shape[0], cache_k.shape[1], cache_k.shape[2]
    past_len = page_table.shape[1] * page
    assert CHUNK % ts == 0 and (bs * ts) % CHUNK == 0 and tp % MOBA_BLOCK == 0

    hp = x_prompt.reshape(bp * tp, d)
    hs = x_sample.reshape(bs * ts, d)
    ck = cache_k.reshape(n_attn * n_pool, page * N_KV_HEADS, HEAD_DIM)
    cv = cache_v.reshape(n_attn * n_pool, page * N_KV_HEADS, HEAD_DIM)
    tab_p = _rope_tables(jnp.arange(tp, dtype=jnp.int32))
    pos_s = past_len + jnp.arange(ts, dtype=jnp.int32)
    tab_s = tuple(jnp.tile(t, (bs, 1)) for t in _rope_tables(pos_s))

    pad = LANES - N_EXPERT_GROUPS - N_EXPERTS
    wr_all = jnp.concatenate(
        [moe_w_group, moe_w_router, jnp.zeros((depth, d, pad), F32)], axis=2)
    br_all = jnp.concatenate(
        [moe_b_group, moe_b_router, jnp.zeros((depth, pad), F32)], axis=1)[:, None, :]

    new_k_p, new_v_p, new_k_s, new_v_s, new_chunk_v = [], [], [], [], []
    reps = CHUNK // ts
    for i in range(depth):
        j = i // 2
        n1 = norm1[i][None, :]
        if i % 2 == 0:
            g_v = a_g_v[j][None, :]
            hp = _mixer_a(hp, n1, a_w_in[j].astype(BF16), g_v, a_w_s[j], a_b_s[j][:, :, None],
                          a_w_out[j].astype(BF16), tm=512, seq_len=CHUNK, emit_v=False,
                          precise=False)
            wmix_s = jnp.tile(a_w_s[j][:, :ts, :ts], (1, reps, reps))
            bs_s = jnp.tile(a_b_s[j][:, :ts], (1, reps))[:, :, None]
            hs, v_rows = _mixer_a(hs, n1, a_w_in[j], g_v, wmix_s, bs_s, a_w_out[j],
                                  tm=CHUNK, seq_len=ts, emit_v=True, precise=True)
            new_chunk_v.append(v_rows.reshape(bs, ts, -1))
        else:
            qn = b_q_norm[j][None, :]
            kn = b_k_norm[j][None, :]
            qp, kp, vp = _qkv(hp, n1, b_w_qkv[j].astype(BF16), qn, kn, tab_p, tm=512,
                              precise=False)
            qs, ks_new, vs_new = _qkv(hs, n1, b_w_qkv[j], qn, kn, tab_s, tm=bs * ts, precise=True)
            hp = _attn_prompt(qp, kp, vp, hp, b_w_o[j].astype(BF16), batch=bp, seq=tp)
            o_s = _attn_sample(qs, ks_new, vs_new, ck, cv, page_table, j, n_pool=n_pool,
                               batch=bs, dec_seq=ts)
            hs = _oproj(o_s, hs, b_w_o[j])
            new_k_p.append(kp.reshape(bp, tp, N_KV_HEADS, HEAD_DIM))
            new_v_p.append(vp.reshape(bp, tp, N_KV_HEADS, HEAD_DIM))
            new_k_s.append(ks_new.reshape(bs, ts, N_KV_HEADS, HEAD_DIM))
            new_v_s.append(vs_new.reshape(bs, ts, N_KV_HEADS, HEAD_DIM))
        n2 = norm2[i][None, :]
        hp = _moe(hp, n2, wr_all[i], br_all[i], moe_w_gate[i].astype(BF16),
                  moe_w_up[i].astype(BF16), moe_w_down[i].astype(BF16),
                  tm=min(1024, bp * tp), precise=False)
        hs = _moe(hs, n2, wr_all[i], br_all[i], moe_w_gate[i], moe_w_up[i], moe_w_down[i],
                  tm=bs * ts, precise=True)
    return (hp.reshape(bp, tp, d), hs.reshape(bs, ts, d),
            jnp.stack(new_k_p), jnp.stack(new_v_p), jnp.stack(new_k_s), jnp.stack(new_v_s),
            jnp.stack(new_chunk_v))
```

```python
import functools

import jax
import jax.numpy as jnp
from jax import lax
from jax.experimental import pallas as pl
from jax.experimental.pallas import tpu as pltpu

F32 = jnp.float32
BF16 = jnp.bfloat16

CHUNK = 128
CHUNK_GROUPS = 4
HEAD_DIM = 128
N_HEADS = 8
N_KV_HEADS = 4
KV_GROUP = N_HEADS // N_KV_HEADS
ROT_DIM = HEAD_DIM // 4
ROPE_THETA = 500000.0
MOBA_BLOCK = 256
MOBA_TOP_K = 3
N_EXPERT_GROUPS = 4
EXPERTS_PER_GROUP = 4
N_EXPERTS = N_EXPERT_GROUPS * EXPERTS_PER_GROUP
NORM_EPS = 1e-6
NEG_INF = -1e30
LOG2E = 1.4426950408889634
SPAN_BLOCKS = 2

SUBLANES = 8
LANES = 128
VMEM_LIMIT_BYTES = 56 * 1024 * 1024

_NT = (((1,), (1,)), ((), ()))


def _dot(a, b):
    return jnp.dot(a, b, preferred_element_type=F32)


def _dot_nt(a, b):
    return lax.dot_general(a, b, _NT, preferred_element_type=F32)


def _split_bf16(a):
    hi = a.astype(BF16)
    lo = (a - hi.astype(F32)).astype(BF16)
    return hi, lo


def _dot3(a, b, nt=False):
    d = _dot_nt if nt else _dot
    ah, al = _split_bf16(a)
    bh, bl = _split_bf16(b)
    return d(ah, bh) + (d(ah, bl) + d(al, bh))


def _mm(x, w, precise):
    if precise:
        return _dot3(x, w)
    return _dot(x.astype(BF16), w.astype(BF16))


def _rms(x, g):
    ms = jnp.mean(x * x, axis=-1, keepdims=True)
    return x * lax.rsqrt(ms + NORM_EPS) * g


def _gelu_tanh(x):
    c = 0.7978845608028654
    return 0.5 * x * (1.0 + jnp.tanh(c * (x + 0.044715 * (x * x * x))))


def _silu(x):
    return x * (1.0 / (1.0 + jnp.exp(-x)))


def _cparams(sem):
    return pltpu.CompilerParams(dimension_semantics=sem, vmem_limit_bytes=VMEM_LIMIT_BYTES)


def _mixer_a_kernel(h_ref, n1_ref, win_ref, gv_ref, wmix_ref, bs_ref, wout_ref, *rest,
                    seq_len, emit_v, precise):
    if emit_v:
        out_ref, v_ref, us_scr = rest
    else:
        out_ref, us_scr = rest
    tm = h_ref.shape[0]
    width = gv_ref.shape[1]
    gdim = width // CHUNK_GROUPS
    x = h_ref[...]
    xn = _rms(x, n1_ref[...])
    z = _gelu_tanh(_mm(xn, win_ref[...], precise))
    u = z[:, :width]
    vn = _rms(z[:, width:], gv_ref[...])
    if emit_v:
        v_ref[...] = vn
    row = lax.broadcasted_iota(jnp.int32, (CHUNK, CHUNK), 0)
    col = lax.broadcasted_iota(jnp.int32, (CHUNK, CHUNK), 1)
    mask = col <= row
    if seq_len < CHUNK:
        mask = mask & ((row // seq_len) == (col // seq_len))
    if not precise:
        vn = vn.astype(BF16)
    for g in range(CHUNK_GROUPS):
        wm = jnp.where(mask, wmix_ref[g], 0.0)
        bias = bs_ref[g]
        for c in range(tm // CHUNK):
            rows = slice(c * CHUNK, (c + 1) * CHUNK)
            cols = slice(g * gdim, (g + 1) * gdim)
            s = _mm(wm, vn[rows, cols], precise) + bias
            us_scr[rows, cols] = (u[rows, cols] * s).astype(us_scr.dtype)
    out_ref[...] = x + _mm(us_scr[...], wout_ref[...], precise)


def _mixer_a(h, n1, w_in, g_v, wmix, bs, w_out, layer, *, tm, seq_len, emit_v, precise):
    n, d = h.shape
    width = g_v.shape[1]
    const = lambda i: (0, 0)
    out_shape = [jax.ShapeDtypeStruct((n, d), F32)]
    out_specs = [pl.BlockSpec((tm, d), lambda i: (i, 0))]
    if emit_v:
        out_shape.append(jax.ShapeDtypeStruct((n, width), F32))
        out_specs.append(pl.BlockSpec((tm, width), lambda i: (i, 0)))
    res = pl.pallas_call(
        functools.partial(_mixer_a_kernel, seq_len=seq_len, emit_v=emit_v, precise=precise),
        grid=(n // tm,),
        in_specs=[
            pl.BlockSpec((tm, d), lambda i: (i, 0)),
            pl.BlockSpec((1, d), const),
            _layer_spec((d, 2 * width), layer),
            pl.BlockSpec((1, width), const),
            pl.BlockSpec((CHUNK_GROUPS, CHUNK, CHUNK), lambda i: (0, 0, 0)),
            pl.BlockSpec((CHUNK_GROUPS, CHUNK, 1), lambda i: (0, 0, 0)),
            _layer_spec((width, d), layer),
        ],
        out_specs=out_specs,
        out_shape=out_shape,
        scratch_shapes=[pltpu.VMEM((tm, width), F32 if precise else BF16)],
        compiler_params=_cparams(("parallel",)),
        name="mixer_a_sample" if emit_v else "mixer_a",
    )(h, n1, w_in, g_v, wmix, bs, w_out)
    return res if emit_v else res[0]


def _qkv_kernel(h_ref, n1_ref, w_ref, qn_ref, kn_ref, cos_ref, sa_ref, sb_ref,
                q_ref, k_ref, v_ref, *, precise):
    xn = _rms(h_ref[...], n1_ref[...])
    qkv = _mm(xn, w_ref[...], precise)
    cos = cos_ref[...]
    sa = sa_ref[...]
    sb = sb_ref[...]
    nq = q_ref.shape[1]
    nk = N_KV_HEADS * HEAD_DIM

    def head(t, g):
        t = _rms(t, g)
        return (t * cos + pltpu.roll(t, HEAD_DIM - ROT_DIM // 2, 1) * sa
                + pltpu.roll(t, ROT_DIM // 2, 1) * sb)

    tm = h_ref.shape[0]
    for hd in range(nq // HEAD_DIM):
        cols = slice(hd * HEAD_DIM, (hd + 1) * HEAD_DIM)
        q_ref[:, cols] = head(qkv[:, cols], qn_ref[...])
    for hd in range(N_KV_HEADS):
        rows = pl.ds(hd, tm, stride=N_KV_HEADS)
        k_ref[rows, :] = head(qkv[:, nq + hd * HEAD_DIM: nq + (hd + 1) * HEAD_DIM], kn_ref[...])
        v_ref[rows, :] = qkv[:, nq + nk + hd * HEAD_DIM: nq + nk + (hd + 1) * HEAD_DIM]


def _rope_tables(pos):
    half = ROT_DIM // 2
    inv_freq = jnp.power(ROPE_THETA, -jnp.arange(half, dtype=F32) * 2.0 / ROT_DIM)
    ang = pos.astype(F32)[:, None] * inv_freq[None, :]
    cos = jnp.cos(ang)
    sin = jnp.sin(ang)
    t = pos.shape[0]
    ones = jnp.ones((t, HEAD_DIM - ROT_DIM), F32)
    zeros = jnp.zeros((t, HEAD_DIM - ROT_DIM), F32)
    z16 = jnp.zeros((t, half), F32)
    cos_t = jnp.concatenate([cos, cos, ones], axis=1)
    sa_t = jnp.concatenate([-sin, z16, zeros], axis=1)
    sb_t = jnp.concatenate([z16, sin, zeros], axis=1)
    return cos_t, sa_t, sb_t


def _layer_spec(shape, layer):
    zeros = (0,) * len(shape)
    return pl.BlockSpec((None,) + tuple(shape), lambda *_: (layer,) + zeros)


def _qkv(h, n1, w, layer, qn, kn, tables, *, tm, precise):
    n, d = h.shape
    nq = N_HEADS * HEAD_DIM
    nk = N_KV_HEADS * HEAD_DIM
    t = tables[0].shape[0]
    tiles_per_seq = t // tm
    const = lambda i: (0, 0)
    tab = pl.BlockSpec((tm, HEAD_DIM), lambda i: (i % tiles_per_seq, 0))
    return pl.pallas_call(
        functools.partial(_qkv_kernel, precise=precise),
        grid=(n // tm,),
        in_specs=[
            pl.BlockSpec((tm, d), lambda i: (i, 0)),
            pl.BlockSpec((1, d), const),
            _layer_spec((d, nq + 2 * nk), layer),
            pl.BlockSpec((1, HEAD_DIM), const),
            pl.BlockSpec((1, HEAD_DIM), const),
            tab, tab, tab,
        ],
        out_specs=[
            pl.BlockSpec((tm, nq), lambda i: (i, 0)),
            pl.BlockSpec((tm * N_KV_HEADS, HEAD_DIM), lambda i: (i, 0)),
            pl.BlockSpec((tm * N_KV_HEADS, HEAD_DIM), lambda i: (i, 0)),
        ],
        out_shape=[
            jax.ShapeDtypeStruct((n, nq), F32),
            jax.ShapeDtypeStruct((n * N_KV_HEADS, HEAD_DIM), F32),
            jax.ShapeDtypeStruct((n * N_KV_HEADS, HEAD_DIM), F32),
        ],
        compiler_params=_cparams(("parallel",)),
        name="qkv_sample" if precise else "qkv",
    )(h, n1, w, qn, kn, *tables)


def _attn_prompt_kernel(q_ref, k_ref, v_ref, h_ref, wo_ref, out_ref,
                        kbf, vtb, kmean, s_scr, p_scr, o_scr):
    blk = MOBA_BLOCK
    nblk = kbf.shape[0] // blk
    own = pl.program_id(1)
    scale = HEAD_DIM ** -0.5

    @pl.when(own == 0)
    def _():
        for n in range(nblk):
            rows = slice(n * blk, (n + 1) * blk)
            for kh in range(N_KV_HEADS):
                kcols = slice(kh * HEAD_DIM, (kh + 1) * HEAD_DIM)
                src_rows = pl.ds(n * blk * N_KV_HEADS + kh, blk, stride=N_KV_HEADS)
                kk = k_ref[src_rows, :]
                kbf[rows, kcols] = kk.astype(BF16)
                kmean[n:n + 1, kcols] = jnp.mean(kk, axis=0, keepdims=True)
                vtb[kcols, rows] = v_ref[src_rows, :].T.astype(BF16)

    cols = KV_GROUP * blk
    nidx = lax.broadcasted_iota(jnp.int32, (nblk, cols), 0)
    kq_diff = (lax.broadcasted_iota(jnp.int32, (blk, cols), 0)
               - lax.broadcasted_iota(jnp.int32, (blk, cols), 1) % blk)

    for kh in range(N_KV_HEADS):
        kcols = slice(kh * HEAD_DIM, (kh + 1) * HEAD_DIM)
        q2 = jnp.concatenate(
            [q_ref[:, (KV_GROUP * kh + i) * HEAD_DIM:(KV_GROUP * kh + i + 1) * HEAD_DIM]
             for i in range(KV_GROUP)], axis=0)
        q2s = (q2 * (scale * LOG2E)).astype(BF16)

        g_t = _dot3(kmean[:, kcols], q2, nt=True)
        rank = jnp.zeros_like(g_t)
        for m in range(nblk - 1):
            gm = g_t[m:m + 1, :]
            beats = (gm > g_t) | ((gm == g_t) & (nidx > m))
            inc = jnp.where(m < own, 1.0, 0.0)
            rank = rank + jnp.where(beats, inc, 0.0)
        sel_t = jnp.where(nidx == own, 1.0,
                          jnp.where((nidx < own) & (rank < float(MOBA_TOP_K)), 1.0, 0.0))

        def attend(nb, kh=kh, kcols=kcols, q2s=q2s, sel_t=sel_t):
            m = jnp.full((1, cols), NEG_INF, F32)
            for n in range(nb):
                rows = slice(n * blk, (n + 1) * blk)
                s = _dot_nt(kbf[rows, kcols], q2s)
                s = jnp.where(sel_t[n:n + 1, :] > 0.5, s, NEG_INF)
                if n >= nb - SPAN_BLOCKS:
                    s = jnp.where(kq_diff <= (own - n) * blk, s, NEG_INF)
                s_scr[rows, :] = s
                m = jnp.maximum(m, jnp.max(s, axis=0, keepdims=True))
            l = jnp.zeros((1, cols), F32)
            for n in range(nb):
                rows = slice(n * blk, (n + 1) * blk)
                p = jnp.exp2(s_scr[rows, :] - m)
                l = l + jnp.sum(p, axis=0, keepdims=True)
                p_scr[rows, :] = p.astype(BF16)
            acc = _dot(vtb[kcols, :nb * blk], p_scr[:nb * blk, :])
            o = (acc / l).T.astype(BF16)
            for i in range(KV_GROUP):
                hd = KV_GROUP * kh + i
                o_scr[:, hd * HEAD_DIM:(hd + 1) * HEAD_DIM] = o[i * blk:(i + 1) * blk]

        for nb in range(SPAN_BLOCKS, nblk + SPAN_BLOCKS, SPAN_BLOCKS):
            pl.when((own >= nb - SPAN_BLOCKS) & (own < nb))(functools.partial(attend, min(nb, nblk)))

    out_ref[...] = h_ref[...] + _dot(o_scr[...], wo_ref[...])


def _attn_prompt(q, k, v, h, wo, layer, *, batch, seq):
    blk = MOBA_BLOCK
    nqb = seq // blk
    d = h.shape[1]
    nq = q.shape[1]
    nk = N_KV_HEADS * HEAD_DIM
    cols = KV_GROUP * blk
    return pl.pallas_call(
        _attn_prompt_kernel,
        grid=(batch, nqb),
        in_specs=[
            pl.BlockSpec((blk, nq), lambda b, i: (b * nqb + i, 0)),
            pl.BlockSpec((seq * N_KV_HEADS, HEAD_DIM), lambda b, i: (b, 0)),
            pl.BlockSpec((seq * N_KV_HEADS, HEAD_DIM), lambda b, i: (b, 0)),
            pl.BlockSpec((blk, d), lambda b, i: (b * nqb + i, 0)),
            _layer_spec((nq, d), layer),
        ],
        out_specs=pl.BlockSpec((blk, d), lambda b, i: (b * nqb + i, 0)),
        out_shape=jax.ShapeDtypeStruct(h.shape, F32),
        scratch_shapes=[
            pltpu.VMEM((seq, nk), BF16),
            pltpu.VMEM((nk, seq), BF16),
            pltpu.VMEM((nqb, nk), F32),
            pltpu.VMEM((seq, cols), F32),
            pltpu.VMEM((seq, cols), BF16),
            pltpu.VMEM((blk, nq), BF16),
        ],
        compiler_params=_cparams(("parallel", "arbitrary")),
        name="attn_prompt",
    )(q, k, v, h, wo)


def _attn_sample_kernel(pt_ref, qd_ref, knew_ref, vnew_ref, ck_hbm, cv_hbm, out_ref,
                        kbuf, vbuf, s_scr, p_scr, kmean_scr, ksem, vsem,
                        *, base, npg, dec_seq):
    b = pl.program_id(0)
    nbatch = pl.num_programs(0)
    n_pages, prow, _ = kbuf.shape
    nchunk = n_pages // npg
    ccols = npg * prow
    bcols = MOBA_BLOCK * N_KV_HEADS
    pages_per_blk = bcols // prow
    blk_per_chunk = npg // pages_per_blk
    nrow = qd_ref.shape[0]
    nl = kmean_scr.shape[0]
    nblk = nl // SUBLANES
    scale = HEAD_DIM ** -0.5

    def page_copy(hbm, buf, sem, bb, pg):
        return pltpu.make_async_copy(hbm.at[base + pt_ref[bb, pg]], buf.at[pg], sem.at[pg // npg])

    def fetch(hbm, buf, sem, bb):
        for pg in range(n_pages):
            page_copy(hbm, buf, sem, bb, pg).start()

    def wait_chunk(hbm, buf, sem, c):
        for pg in range(c * npg, (c + 1) * npg):
            page_copy(hbm, buf, sem, b, pg).wait()

    @pl.when(b == 0)
    def _():
        fetch(ck_hbm, kbuf, ksem, 0)
        fetch(cv_hbm, vbuf, vsem, 0)

    qd = qd_ref[...]
    q_hi, q_lo = _split_bf16(qd)
    q_hl = jnp.concatenate([q_hi, q_lo], axis=0)

    for c in range(nchunk):
        wait_chunk(ck_hbm, kbuf, ksem, c)
        kf = [kbuf[c * npg + i] for i in range(npg)]
        for i in range(blk_per_chunk):
            tot = kf[pages_per_blk * i]
            for j in range(1, pages_per_blk):
                tot = tot + kf[pages_per_blk * i + j]
            part = jnp.sum(tot.reshape(prow // SUBLANES, SUBLANES, HEAD_DIM), axis=0)
            part = (part + pltpu.roll(part, N_KV_HEADS, 0)) * (1.0 / MOBA_BLOCK)
            row0 = (c * blk_per_chunk + i) * SUBLANES
            kmean_scr[row0:row0 + SUBLANES, :] = part
        k_hi, k_lo = _split_bf16(jnp.concatenate(kf, axis=0))
        s2 = _dot_nt(q_hl, k_hi)
        s_scr[:, c * ccols:(c + 1) * ccols] = (s2[:nrow] + s2[nrow:]) + _dot_nt(q_hi, k_lo)

    @pl.when(b + 1 < nbatch)
    def _():
        fetch(ck_hbm, kbuf, ksem, b + 1)

    g = _dot3(qd, kmean_scr[...], nt=True)
    lane = lax.broadcasted_iota(jnp.int32, (nrow, nl), 1)
    kv_r = (lax.broadcasted_iota(jnp.int32, (nrow, nl), 0) // dec_seq) // KV_GROUP
    rowsel = [jnp.where(kv_r == kh, 1.0, 0.0) for kh in range(N_KV_HEADS)]
    rank = jnp.zeros_like(g)
    for m in range(nblk):
        for kh in range(N_KV_HEADS):
            pos = m * SUBLANES + kh
            col = g[:, pos:pos + 1]
            beats = (col > g) | ((col == g) & (lane > pos))
            rank = rank + jnp.where(beats, rowsel[kh], 0.0)
    sel = jnp.where((lane % SUBLANES) == kv_r,
                    jnp.where(rank < float(MOBA_TOP_K), 1.0, 0.0), 0.0)
    blkid = lane // SUBLANES

    kv_c = (lax.broadcasted_iota(jnp.int32, (nrow, 1), 0) // dec_seq) // KV_GROUP
    t_row = lax.broadcasted_iota(jnp.int32, (nrow, 1), 0) % dec_seq

    def rows_for(x, t):
        out = jnp.zeros((nrow, HEAD_DIM), F32)
        for kh in range(N_KV_HEADS):
            r = t * N_KV_HEADS + kh
            out = out + jnp.where(kv_c == kh, x[r:r + 1, :], 0.0)
        return out

    knew = knew_ref[...]
    vnew = vnew_ref[...]
    s_own = [jnp.sum(qd * rows_for(knew, t), axis=-1, keepdims=True) * scale
             for t in range(dec_seq)]
    ok_own = [t_row >= t for t in range(dec_seq)]

    kh_match = (lax.broadcasted_iota(jnp.int32, (nrow, bcols), 1) % N_KV_HEADS) == (
        (lax.broadcasted_iota(jnp.int32, (nrow, bcols), 0) // dec_seq) // KV_GROUP)
    m = jnp.full((nrow, 1), NEG_INF, F32)
    for t in range(dec_seq):
        m = jnp.maximum(m, jnp.where(ok_own[t], s_own[t], NEG_INF))
    for n in range(nblk):
        cs = slice(n * bcols, (n + 1) * bcols)
        pick = jnp.max(jnp.where(blkid == n, sel, 0.0), axis=1, keepdims=True)
        sc = jnp.where(pick > 0.5, s_scr[:, cs] * scale, NEG_INF)
        sc = jnp.where(kh_match, sc, NEG_INF)
        s_scr[:, cs] = sc
        m = jnp.maximum(m, jnp.max(sc, axis=-1, keepdims=True))
    l = jnp.zeros((nrow, 1), F32)
    for n in range(nblk):
        cs = slice(n * bcols, (n + 1) * bcols)
        p = jnp.exp(s_scr[:, cs] - m)
        l = l + jnp.sum(p, axis=-1, keepdims=True)
        p_hi, p_lo = _split_bf16(p)
        p_scr[:nrow, cs] = p_hi
        p_scr[nrow:, cs] = p_lo
    acc = jnp.zeros((nrow, HEAD_DIM), F32)
    for t in range(dec_seq):
        p_t = jnp.where(ok_own[t], jnp.exp(s_own[t] - m), 0.0)
        l = l + p_t
        acc = acc + p_t * rows_for(vnew, t)

    for c in range(nchunk):
        wait_chunk(cv_hbm, vbuf, vsem, c)
        vb = jnp.concatenate([vbuf[c * npg + i].astype(BF16) for i in range(npg)], axis=0)
        pv = _dot(p_scr[:, c * ccols:(c + 1) * ccols], vb)
        acc = acc + (pv[:nrow] + pv[nrow:])

    @pl.when(b + 1 < nbatch)
    def _():
        fetch(cv_hbm, vbuf, vsem, b + 1)

    out_ref[...] = acc / l


def _attn_sample(q, k_new, v_new, cache_k, cache_v, page_table, layer, *, n_pool, batch, dec_seq,
                 npg=8):
    n_pages = page_table.shape[1]
    prow = cache_k.shape[1]
    page = prow // N_KV_HEADS
    assert (n_pages * page) % MOBA_BLOCK == 0 and dec_seq <= MOBA_BLOCK
    assert n_pages % npg == 0 and MOBA_BLOCK % page == 0 and npg % (MOBA_BLOCK // page) == 0
    assert SUBLANES == 2 * N_KV_HEADS and prow % SUBLANES == 0
    nchunk = n_pages // npg
    nblk = n_pages * page // MOBA_BLOCK
    nrow = N_HEADS * dec_seq
    qd = q.reshape(batch, dec_seq, N_HEADS, HEAD_DIM).transpose(0, 2, 1, 3)
    qd = qd.reshape(batch, nrow, HEAD_DIM)
    per_b = lambda b, pt: (b, 0, 0)
    new_rows = dec_seq * N_KV_HEADS
    grid_spec = pltpu.PrefetchScalarGridSpec(
        num_scalar_prefetch=1,
        grid=(batch,),
        in_specs=[
            pl.BlockSpec((None, nrow, HEAD_DIM), per_b),
            pl.BlockSpec((None, new_rows, HEAD_DIM), per_b),
            pl.BlockSpec((None, new_rows, HEAD_DIM), per_b),
            pl.BlockSpec(memory_space=pl.ANY),
            pl.BlockSpec(memory_space=pl.ANY),
        ],
        out_specs=pl.BlockSpec((None, nrow, HEAD_DIM), per_b),
        scratch_shapes=[
            pltpu.VMEM((n_pages, prow, HEAD_DIM), F32),
            pltpu.VMEM((n_pages, prow, HEAD_DIM), F32),
            pltpu.VMEM((nrow, n_pages * prow), F32),
            pltpu.VMEM((2 * nrow, n_pages * prow), BF16),
            pltpu.VMEM((nblk * SUBLANES, HEAD_DIM), F32),
            pltpu.SemaphoreType.DMA((nchunk,)),
            pltpu.SemaphoreType.DMA((nchunk,)),
        ],
    )
    o = pl.pallas_call(
        functools.partial(_attn_sample_kernel, base=layer * n_pool, npg=npg, dec_seq=dec_seq),
        grid_spec=grid_spec,
        out_shape=jax.ShapeDtypeStruct((batch, nrow, HEAD_DIM), F32),
        compiler_params=_cparams(("arbitrary",)),
        name="attn_sample",
    )(page_table, qd, k_new.reshape(batch, new_rows, HEAD_DIM),
      v_new.reshape(batch, new_rows, HEAD_DIM), cache_k, cache_v)
    o = o.reshape(batch, N_HEADS, dec_seq, HEAD_DIM).transpose(0, 2, 1, 3)
    return o.reshape(batch * dec_seq, N_HEADS * HEAD_DIM)


def _oproj_kernel(o_ref, h_ref, wo_ref, out_ref):
    out_ref[...] = h_ref[...] + _dot3(o_ref[...], wo_ref[...])


def _oproj(o, h, wo, layer):
    n, d = h.shape
    full = pl.BlockSpec((n, d), lambda i: (0, 0))
    return pl.pallas_call(
        _oproj_kernel,
        grid=(1,),
        in_specs=[pl.BlockSpec(o.shape, lambda i: (0, 0)), full,
                  _layer_spec(wo.shape[1:], layer)],
        out_specs=full,
        out_shape=jax.ShapeDtypeStruct(h.shape, F32),
        compiler_params=_cparams(("arbitrary",)),
        name="oproj_sample",
    )(o, h, wo)


def _moe_kernel(h_ref, n2_ref, wr_ref, br_ref, wg_ref, wu_ref, wd_ref, out_ref,
                xn_scr, gates_scr, acc_scr, *, precise):
    g = pl.program_id(1)
    ngroups = pl.num_programs(1)
    tm = h_ref.shape[0]
    lane = lax.broadcasted_iota(jnp.int32, (tm, LANES), 1)
    first_expert_lane = N_EXPERT_GROUPS

    @pl.when(g == 0)
    def _():
        x = h_ref[...]
        xn = _rms(x, n2_ref[...])
        xn_scr[...] = xn.astype(xn_scr.dtype)
        acc_scr[...] = x
        _, gates_scr[...] = _route(_dot3(xn, wr_ref[...]) + br_ref[...], lane)

    xb = xn_scr[...]
    gates = gates_scr[...]
    acc = acc_scr[...]
    for e in range(EXPERTS_PER_GROUP):
        a = _mm(xb, wg_ref[e], precise)
        b = _mm(xb, wu_ref[e], precise)
        tgt = first_expert_lane + g * EXPERTS_PER_GROUP + e
        gate = jnp.sum(jnp.where(lane == tgt, gates, 0.0), axis=-1, keepdims=True)
        hid = _silu(a) * b * gate
        acc = acc + _mm(hid, wd_ref[e], precise)
    acc_scr[...] = acc

    @pl.when(g == ngroups - 1)
    def _():
        out_ref[...] = acc_scr[...]


def _moe(h, n2, wr, br, wg, wu, wd, layer, *, tm, precise):
    n, d = h.shape
    de = wg.shape[3]
    epg = EXPERTS_PER_GROUP
    const = lambda i, g: (0, 0)
    group = lambda i, g: (layer, g, 0, 0)
    return pl.pallas_call(
        functools.partial(_moe_kernel, precise=precise),
        grid=(n // tm, N_EXPERT_GROUPS),
        in_specs=[
            pl.BlockSpec((tm, d), lambda i, g: (i, 0)),
            pl.BlockSpec((1, d), const),
            pl.BlockSpec((d, LANES), const),
            pl.BlockSpec((1, LANES), const),
            pl.BlockSpec((None, epg, d, de), group),
            pl.BlockSpec((None, epg, d, de), group),
            pl.BlockSpec((None, epg, de, d), group),
        ],
        out_specs=pl.BlockSpec((tm, d), lambda i, g: (i, 0)),
        out_shape=jax.ShapeDtypeStruct((n, d), F32),
        scratch_shapes=[
            pltpu.VMEM((tm, d), F32 if precise else BF16),
            pltpu.VMEM((tm, LANES), F32),
            pltpu.VMEM((tm, d), F32),
        ],
        compiler_params=_cparams(("parallel", "arbitrary")),
        name="moe_sample" if precise else "moe",
    )(h, n2, wr, br, wg, wu, wd)


def _route(logits, lane):
    first_expert_lane = N_EXPERT_GROUPS
    is_group = lane < N_EXPERT_GROUPS
    gl = jnp.where(is_group, logits, NEG_INF)
    ge = jnp.where(is_group, jnp.exp(gl - jnp.max(gl, axis=-1, keepdims=True)), 0.0)
    gprob = ge / jnp.sum(ge, axis=-1, keepdims=True)
    g_w = jnp.max(gprob, axis=-1, keepdims=True)
    g_idx = jnp.min(jnp.where(is_group & (gprob == g_w), lane, LANES), axis=-1, keepdims=True)
    lo = first_expert_lane + g_idx * EXPERTS_PER_GROUP
    in_group = (lane >= lo) & (lane < lo + EXPERTS_PER_GROUP)
    el = jnp.where(in_group, logits, NEG_INF)
    ee = jnp.where(in_group, jnp.exp(el - jnp.max(el, axis=-1, keepdims=True)), 0.0)
    eprob = jnp.where(in_group, ee / jnp.sum(ee, axis=-1, keepdims=True), -1.0)
    p1 = jnp.max(eprob, axis=-1, keepdims=True)
    i1 = jnp.min(jnp.where(eprob == p1, lane, LANES), axis=-1, keepdims=True)
    rest = jnp.where(lane == i1, -1.0, eprob)
    p2 = jnp.max(rest, axis=-1, keepdims=True)
    i2 = jnp.min(jnp.where(rest == p2, lane, LANES), axis=-1, keepdims=True)
    denom = p1 + p2
    w1 = p1 / denom * g_w
    w2 = p2 / denom * g_w
    return g_idx, jnp.where(lane == i1, w1, 0.0) + jnp.where(lane == i2, w2, 0.0)


def _route_kernel(h_ref, n2_ref, wr_ref, br_ref, hx_ref, meta_ref, cnt_ref, run_scr, ltri_scr):
    i = pl.program_id(0)
    tm, d = h_ref.shape

    @pl.when(i == 0)
    def _():
        run_scr[...] = jnp.zeros_like(run_scr)
        r = lax.broadcasted_iota(jnp.int32, (tm, tm), 0)
        c = lax.broadcasted_iota(jnp.int32, (tm, tm), 1)
        ltri_scr[...] = jnp.where(c < r, 1.0, 0.0).astype(BF16)

    lane = lax.broadcasted_iota(jnp.int32, (tm, LANES), 1)
    x = h_ref[...]
    xn = _rms(x, n2_ref[...])
    g_idx, gates = _route(_dot3(xn, wr_ref[...]) + br_ref[...], lane)
    onehot = jnp.where(lane == g_idx, 1.0, 0.0)
    before = _dot(ltri_scr[...], onehot.astype(BF16)) + run_scr[...]
    rank = jnp.sum(onehot * before, axis=-1, keepdims=True)
    run_scr[...] += jnp.sum(onehot, axis=0, keepdims=True)
    info = (gates + jnp.where(lane == 0, g_idx.astype(F32), 0.0)
            + jnp.where(lane == 1, rank, 0.0))
    hx_ref[:, :d] = x
    hx_ref[:, d:] = info
    rr = lax.broadcasted_iota(jnp.int32, (SUBLANES, LANES), 0)
    ll = lax.broadcasted_iota(jnp.int32, (SUBLANES, LANES), 1)
    pick = jnp.where((rr == ll) & (rr < 2), 1.0, 0.0).astype(BF16)
    ih, il = _split_bf16(info)
    meta_ref[...] = _dot_nt(pick, ih) + _dot_nt(pick, il)
    cnt_ref[...] = jnp.broadcast_to(run_scr[...], cnt_ref.shape)


def _route_call(h, n2, wr, br, *, tm):
    n, d = h.shape
    const = lambda i: (0, 0)
    return pl.pallas_call(
        _route_kernel,
        grid=(n // tm,),
        in_specs=[
            pl.BlockSpec((tm, d), lambda i: (i, 0)),
            pl.BlockSpec((1, d), const),
            pl.BlockSpec((d, LANES), const),
            pl.BlockSpec((1, LANES), const),
        ],
        out_specs=[
            pl.BlockSpec((tm, d + LANES), lambda i: (i, 0)),
            pl.BlockSpec((SUBLANES, tm), lambda i: (0, i)),
            pl.BlockSpec((SUBLANES, LANES), const),
        ],
        out_shape=[
            jax.ShapeDtypeStruct((n, d + LANES), F32),
            jax.ShapeDtypeStruct((SUBLANES, n), F32),
            jax.ShapeDtypeStruct((SUBLANES, LANES), F32),
        ],
        scratch_shapes=[pltpu.VMEM((1, LANES), F32), pltpu.VMEM((tm, tm), BF16)],
        compiler_params=_cparams(("arbitrary",)),
        name="moe_route",
    )(h, n2, wr, br)


def _permute_kernel(idx_ref, pad_start_ref, pad_cnt_ref, src, zero, dst, sem, pad_sem,
                    *, n, chunk, scatter):
    nchunks = n // chunk

    def row_copy(i, slot):
        j = idx_ref[i]
        s, t = (i, j) if scatter else (j, i)
        return pltpu.make_async_copy(src.at[pl.ds(s, 1)], dst.at[pl.ds(t, 1)], sem.at[slot])

    def issue(c):
        def body(k, carry):
            row_copy(c * chunk + k, c % 2).start()
            return carry
        lax.fori_loop(0, chunk, body, 0, unroll=8)

    def drain(c):
        pltpu.make_async_copy(src.at[pl.ds(0, chunk)], dst.at[pl.ds(0, chunk)],
                              sem.at[c % 2]).wait()

    for c in range(nchunks):
        issue(c)
        if c >= 1:
            drain(c - 1)
    drain(nchunks - 1)

    if scatter:
        def pad_copy(g, k):
            return pltpu.make_async_copy(zero.at[pl.ds(0, 1)],
                                         dst.at[pl.ds(pad_start_ref[g] + k, 1)], pad_sem)

        for g in range(N_EXPERT_GROUPS):
            def start_body(k, carry, g=g):
                pad_copy(g, k).start()
                return carry

            def wait_body(k, carry, g=g):
                pad_copy(g, k).wait()
                return carry

            lax.fori_loop(0, pad_cnt_ref[g], start_body, 0)
            lax.fori_loop(0, pad_cnt_ref[g], wait_body, 0)


def _permute(src, idx, n_out, *, scatter, pad_start=None, pad_cnt=None, chunk=1024):
    n = idx.shape[0]
    w = src.shape[1]
    if pad_start is None:
        pad_start = jnp.zeros((N_EXPERT_GROUPS,), jnp.int32)
        pad_cnt = jnp.zeros((N_EXPERT_GROUPS,), jnp.int32)
    zero = jnp.zeros((SUBLANES, w), F32)
    anyspec = pl.BlockSpec(memory_space=pl.ANY)
    grid_spec = pltpu.PrefetchScalarGridSpec(
        num_scalar_prefetch=3,
        grid=(1,),
        in_specs=[anyspec, anyspec],
        out_specs=anyspec,
        scratch_shapes=[pltpu.SemaphoreType.DMA((2,)), pltpu.SemaphoreType.DMA(())],
    )
    return pl.pallas_call(
        functools.partial(_permute_kernel, n=n, chunk=chunk, scatter=scatter),
        grid_spec=grid_spec,
        out_shape=jax.ShapeDtypeStruct((n_out, w), F32),
        compiler_params=pltpu.CompilerParams(dimension_semantics=("arbitrary",)),
        name="moe_dispatch" if scatter else "moe_combine",
    )(idx, pad_start, pad_cnt, src, zero)


def _moe_sorted_kernel(tg_ref, nu_ref, xs_ref, n2_ref, wg_ref, wu_ref, wd_ref, ys_ref):
    t = pl.program_id(0)

    @pl.when(t < nu_ref[0])
    def _():
        tm, d = ys_ref.shape
        x = xs_ref[:, :d]
        info = xs_ref[:, d:]
        xb = _rms(x, n2_ref[...]).astype(BF16)
        lane = lax.broadcasted_iota(jnp.int32, (tm, LANES), 1)
        first = N_EXPERT_GROUPS + tg_ref[t] * EXPERTS_PER_GROUP
        acc = x
        for e in range(EXPERTS_PER_GROUP):
            a = _dot(xb, wg_ref[e])
            b = _dot(xb, wu_ref[e])
            gate = jnp.sum(jnp.where(lane == first + e, info, 0.0), axis=-1, keepdims=True)
            hid = _silu(a) * b * gate
            acc = acc + _dot(hid.astype(BF16), wd_ref[e])
        ys_ref[...] = acc


def _moe_sorted(xs, tile_group, n_used, n2, wg, wu, wd, *, tm):
    s, wx = xs.shape
    d = wx - LANES
    de = wg.shape[2]
    epg = EXPERTS_PER_GROUP
    last = lambda t, tg, nu: (jnp.minimum(t, nu[0] - 1), 0)
    wmap = lambda t, tg, nu: (tg[t], 0, 0)
    grid_spec = pltpu.PrefetchScalarGridSpec(
        num_scalar_prefetch=2,
        grid=(s // tm,),
        in_specs=[
            pl.BlockSpec((tm, wx), last),
            pl.BlockSpec((1, d), lambda t, tg, nu: (0, 0)),
            pl.BlockSpec((epg, d, de), wmap),
            pl.BlockSpec((epg, d, de), wmap),
            pl.BlockSpec((epg, de, d), wmap),
        ],
        out_specs=pl.BlockSpec((tm, d), last),
    )
    return pl.pallas_call(
        _moe_sorted_kernel,
        grid_spec=grid_spec,
        out_shape=jax.ShapeDtypeStruct((s, d), F32),
        compiler_params=_cparams(("arbitrary",)),
        name="moe_sorted",
    )(tile_group, n_used, xs, n2, wg, wu, wd)


def _moe_prompt(h, n2, wr, br, wg, wu, wd, *, tm):
    n, d = h.shape
    ng = N_EXPERT_GROUPS
    t_max = n // tm + ng
    hx, meta, cnt = _route_call(h, n2, wr, br, tm=tm)
    gidx = meta[0].astype(jnp.int32)
    rank = meta[1].astype(jnp.int32)
    counts = cnt[0, :ng].astype(jnp.int32)
    tiles = (counts + tm - 1) // tm
    ends = jnp.cumsum(tiles)
    starts = ends - tiles
    dest = starts[gidx] * tm + rank
    n_used = ends[ng - 1:ng]
    t_ids = jnp.minimum(jnp.arange(t_max, dtype=jnp.int32), n_used - 1)
    tile_group = jnp.sum((t_ids[:, None] >= ends[None, :]).astype(jnp.int32), axis=1)
    xs = _permute(hx, dest, t_max * tm, scatter=True,
                  pad_start=starts * tm + counts, pad_cnt=tiles * tm - counts)
    ys = _moe_sorted(xs, tile_group, n_used, n2, wg, wu, wd, tm=tm)
    return _permute(ys, dest, n, scatter=False)


def kernel(x_prompt, x_sample, cache_k, cache_v, page_table, norm1, norm2, a_w_in, a_g_v, a_w_s, a_b_s, a_w_out, b_w_qkv, b_q_norm, b_k_norm, b_w_o, moe_w_group, moe_b_group, moe_w_router, moe_b_router, moe_w_gate, moe_w_up, moe_w_down):
    bp, tp, d = x_prompt.shape
    bs, ts, _ = x_sample.shape
    depth = norm1.shape[0]
    n_attn, n_pool, page = cache_k.shape[0], cache_k.shape[1], cache_k.shape[2]
    past_len = page_table.shape[1] * page
    assert CHUNK % ts == 0 and (bs * ts) % CHUNK == 0 and tp % MOBA_BLOCK == 0

    hp = x_prompt.reshape(bp * tp, d)
    hs = x_sample.reshape(bs * ts, d)
    ck = cache_k.reshape(n_attn * n_pool, page * N_KV_HEADS, HEAD_DIM)
    cv = cache_v.reshape(n_attn * n_pool, page * N_KV_HEADS, HEAD_DIM)
    tab_p = _rope_tables(jnp.arange(tp, dtype=jnp.int32))
    pos_s = past_len + jnp.arange(ts, dtype=jnp.int32)
    tab_s = tuple(jnp.tile(t, (bs, 1)) for t in _rope_tables(pos_s))

    pad = LANES - N_EXPERT_GROUPS - N_EXPERTS
    wr_all = jnp.concatenate(
        [moe_w_group, moe_w_router, jnp.zeros((depth, d, pad), F32)], axis=2)
    br_all = jnp.concatenate(
        [moe_b_group, moe_b_router, jnp.zeros((depth, pad), F32)], axis=1)[:, None, :]

    w_in_bf, w_out_bf = a_w_in.astype(BF16), a_w_out.astype(BF16)
    w_qkv_bf, w_o_bf = b_w_qkv.astype(BF16), b_w_o.astype(BF16)
    wg_bf, wu_bf, wd_bf = moe_w_gate.astype(BF16), moe_w_up.astype(BF16), moe_w_down.astype(BF16)

    new_k_p, new_v_p, new_k_s, new_v_s, new_chunk_v = [], [], [], [], []
    reps = CHUNK // ts
    for i in range(depth):
        j = i // 2
        n1 = norm1[i][None, :]
        if i % 2 == 0:
            g_v = a_g_v[j][None, :]
            hp = _mixer_a(hp, n1, w_in_bf, g_v, a_w_s[j], a_b_s[j][:, :, None], w_out_bf, j,
                          tm=512, seq_len=CHUNK, emit_v=False, precise=False)
            wmix_s = jnp.tile(a_w_s[j][:, :ts, :ts], (1, reps, reps))
            bs_s = jnp.tile(a_b_s[j][:, :ts], (1, reps))[:, :, None]
            hs, v_rows = _mixer_a(hs, n1, a_w_in, g_v, wmix_s, bs_s, a_w_out, j,
                                  tm=CHUNK, seq_len=ts, emit_v=True, precise=True)
            new_chunk_v.append(v_rows.reshape(bs, ts, -1))
        else:
            qn = b_q_norm[j][None, :]
            kn = b_k_norm[j][None, :]
            qp, kp, vp = _qkv(hp, n1, w_qkv_bf, j, qn, kn, tab_p, tm=512, precise=False)
            qs, ks_new, vs_new = _qkv(hs, n1, b_w_qkv, j, qn, kn, tab_s, tm=bs * ts, precise=True)
            hp = _attn_prompt(qp, kp, vp, hp, w_o_bf, j, batch=bp, seq=tp)
            o_s = _attn_sample(qs, ks_new, vs_new, ck, cv, page_table, j, n_pool=n_pool,
                               batch=bs, dec_seq=ts)
            hs = _oproj(o_s, hs, b_w_o, j)
            new_k_p.append(kp.reshape(bp, tp, N_KV_HEADS, HEAD_DIM))
            new_v_p.append(vp.reshape(bp, tp, N_KV_HEADS, HEAD_DIM))
            new_k_s.append(ks_new.reshape(bs, ts, N_KV_HEADS, HEAD_DIM))
            new_v_s.append(vs_new.reshape(bs, ts, N_KV_HEADS, HEAD_DIM))
        n2 = norm2[i][None, :]
        hp = _moe(hp, n2, wr_all[i], br_all[i], wg_bf, wu_bf, wd_bf, i,
                  tm=min(1024, bp * tp), precise=False)
        hs = _moe(hs, n2, wr_all[i], br_all[i], moe_w_gate, moe_w_up, moe_w_down, i,
                  tm=bs * ts, precise=True)
    return (hp.reshape(bp, tp, d), hs.reshape(bs, ts, d),
            jnp.stack(new_k_p), jnp.stack(new_v_p), jnp.stack(new_k_s), jnp.stack(new_v_s),
            jnp.stack(new_chunk_v))
```

```python
import functools

import jax
import jax.numpy as jnp
from jax import lax
from jax.experimental import pallas as pl
from jax.experimental.pallas import tpu as pltpu

F32 = jnp.float32
BF16 = jnp.bfloat16

CHUNK = 128
CHUNK_GROUPS = 4
HEAD_DIM = 128
N_HEADS = 8
N_KV_HEADS = 4
KV_GROUP = N_HEADS // N_KV_HEADS
ROT_DIM = HEAD_DIM // 4
ROPE_THETA = 500000.0
MOBA_BLOCK = 256
MOBA_TOP_K = 3
N_EXPERT_GROUPS = 4
EXPERTS_PER_GROUP = 4
N_EXPERTS = N_EXPERT_GROUPS * EXPERTS_PER_GROUP
NORM_EPS = 1e-6
NEG_INF = -1e30
LOG2E = 1.4426950408889634
SPAN_BLOCKS = 2

SUBLANES = 8
LANES = 128
VMEM_LIMIT_BYTES = 56 * 1024 * 1024

_NT = (((1,), (1,)), ((), ()))


def _dot(a, b):
    return jnp.dot(a, b, preferred_element_type=F32)


def _dot_nt(a, b):
    return lax.dot_general(a, b, _NT, preferred_element_type=F32)


def _split_bf16(a):
    hi = a.astype(BF16)
    lo = (a - hi.astype(F32)).astype(BF16)
    return hi, lo


def _dot3(a, b, nt=False):
    d = _dot_nt if nt else _dot
    ah, al = _split_bf16(a)
    bh, bl = _split_bf16(b)
    return d(ah, bh) + (d(ah, bl) + d(al, bh))


def _mm(x, w, precise):
    if precise:
        return _dot3(x, w)
    return _dot(x.astype(BF16), w.astype(BF16))


def _rms(x, g):
    ms = jnp.mean(x * x, axis=-1, keepdims=True)
    return x * lax.rsqrt(ms + NORM_EPS) * g


def _gelu_tanh(x):
    c = 0.7978845608028654
    return 0.5 * x * (1.0 + jnp.tanh(c * (x + 0.044715 * (x * x * x))))


def _silu(x):
    return x * (1.0 / (1.0 + jnp.exp(-x)))


def _cparams(sem):
    return pltpu.CompilerParams(dimension_semantics=sem, vmem_limit_bytes=VMEM_LIMIT_BYTES)


def _mixer_a_kernel(h_ref, n1_ref, win_ref, gv_ref, wmix_ref, bs_ref, wout_ref, *rest,
                    seq_len, emit_v, precise):
    if emit_v:
        out_ref, v_ref, us_scr = rest
    else:
        out_ref, us_scr = rest
    tm = h_ref.shape[0]
    width = gv_ref.shape[1]
    gdim = width // CHUNK_GROUPS
    x = h_ref[...]
    xn = _rms(x, n1_ref[...])
    z = _gelu_tanh(_mm(xn, win_ref[...], precise))
    u = z[:, :width]
    vn = _rms(z[:, width:], gv_ref[...])
    if emit_v:
        v_ref[...] = vn
    row = lax.broadcasted_iota(jnp.int32, (CHUNK, CHUNK), 0)
    col = lax.broadcasted_iota(jnp.int32, (CHUNK, CHUNK), 1)
    mask = col <= row
    if seq_len < CHUNK:
        mask = mask & ((row // seq_len) == (col // seq_len))
    if not precise:
        vn = vn.astype(BF16)
    for g in range(CHUNK_GROUPS):
        wm = jnp.where(mask, wmix_ref[g], 0.0)
        bias = bs_ref[g]
        for c in range(tm // CHUNK):
            rows = slice(c * CHUNK, (c + 1) * CHUNK)
            cols = slice(g * gdim, (g + 1) * gdim)
            s = _mm(wm, vn[rows, cols], precise) + bias
            us_scr[rows, cols] = (u[rows, cols] * s).astype(us_scr.dtype)
    out_ref[...] = x + _mm(us_scr[...], wout_ref[...], precise)


def _mixer_a(h, n1, w_in, g_v, wmix, bs, w_out, layer, *, tm, seq_len, emit_v, precise):
    n, d = h.shape
    width = g_v.shape[1]
    const = lambda i: (0, 0)
    out_shape = [jax.ShapeDtypeStruct((n, d), F32)]
    out_specs = [pl.BlockSpec((tm, d), lambda i: (i, 0))]
    if emit_v:
        out_shape.append(jax.ShapeDtypeStruct((n, width), F32))
        out_specs.append(pl.BlockSpec((tm, width), lambda i: (i, 0)))
    res = pl.pallas_call(
        functools.partial(_mixer_a_kernel, seq_len=seq_len, emit_v=emit_v, precise=precise),
        grid=(n // tm,),
        in_specs=[
            pl.BlockSpec((tm, d), lambda i: (i, 0)),
            pl.BlockSpec((1, d), const),
            _layer_spec((d, 2 * width), layer),
            pl.BlockSpec((1, width), const),
            pl.BlockSpec((CHUNK_GROUPS, CHUNK, CHUNK), lambda i: (0, 0, 0)),
            pl.BlockSpec((CHUNK_GROUPS, CHUNK, 1), lambda i: (0, 0, 0)),
            _layer_spec((width, d), layer),
        ],
        out_specs=out_specs,
        out_shape=out_shape,
        scratch_shapes=[pltpu.VMEM((tm, width), F32 if precise else BF16)],
        compiler_params=_cparams(("parallel",)),
        name="mixer_a_sample" if emit_v else "mixer_a",
    )(h, n1, w_in, g_v, wmix, bs, w_out)
    return res if emit_v else res[0]


def _qkv_kernel(h_ref, n1_ref, w_ref, qn_ref, kn_ref, cos_ref, sa_ref, sb_ref,
                q_ref, k_ref, v_ref, *, precise):
    xn = _rms(h_ref[...], n1_ref[...])
    qkv = _mm(xn, w_ref[...], precise)
    cos = cos_ref[...]
    sa = sa_ref[...]
    sb = sb_ref[...]
    nq = q_ref.shape[1]
    nk = N_KV_HEADS * HEAD_DIM

    def head(t, g):
        t = _rms(t, g)
        return (t * cos + pltpu.roll(t, HEAD_DIM - ROT_DIM // 2, 1) * sa
                + pltpu.roll(t, ROT_DIM // 2, 1) * sb)

    tm = h_ref.shape[0]
    for hd in range(nq // HEAD_DIM):
        cols = slice(hd * HEAD_DIM, (hd + 1) * HEAD_DIM)
        q_ref[:, cols] = head(qkv[:, cols], qn_ref[...])
    for hd in range(N_KV_HEADS):
        rows = pl.ds(hd, tm, stride=N_KV_HEADS)
        k_ref[rows, :] = head(qkv[:, nq + hd * HEAD_DIM: nq + (hd + 1) * HEAD_DIM], kn_ref[...])
        v_ref[rows, :] = qkv[:, nq + nk + hd * HEAD_DIM: nq + nk + (hd + 1) * HEAD_DIM]


def _rope_tables(pos):
    half = ROT_DIM // 2
    inv_freq = jnp.power(ROPE_THETA, -jnp.arange(half, dtype=F32) * 2.0 / ROT_DIM)
    ang = pos.astype(F32)[:, None] * inv_freq[None, :]
    cos = jnp.cos(ang)
    sin = jnp.sin(ang)
    t = pos.shape[0]
    ones = jnp.ones((t, HEAD_DIM - ROT_DIM), F32)
    zeros = jnp.zeros((t, HEAD_DIM - ROT_DIM), F32)
    z16 = jnp.zeros((t, half), F32)
    cos_t = jnp.concatenate([cos, cos, ones], axis=1)
    sa_t = jnp.concatenate([-sin, z16, zeros], axis=1)
    sb_t = jnp.concatenate([z16, sin, zeros], axis=1)
    return cos_t, sa_t, sb_t


def _layer_spec(shape, layer):
    zeros = (0,) * len(shape)
    return pl.BlockSpec((None,) + tuple(shape), lambda *_: (layer,) + zeros)


def _qkv(h, n1, w, layer, qn, kn, tables, *, tm, precise):
    n, d = h.shape
    nq = N_HEADS * HEAD_DIM
    nk = N_KV_HEADS * HEAD_DIM
    t = tables[0].shape[0]
    tiles_per_seq = t // tm
    const = lambda i: (0, 0)
    tab = pl.BlockSpec((tm, HEAD_DIM), lambda i: (i % tiles_per_seq, 0))
    return pl.pallas_call(
        functools.partial(_qkv_kernel, precise=precise),
        grid=(n // tm,),
        in_specs=[
            pl.BlockSpec((tm, d), lambda i: (i, 0)),
            pl.BlockSpec((1, d), const),
            _layer_spec((d, nq + 2 * nk), layer),
            pl.BlockSpec((1, HEAD_DIM), const),
            pl.BlockSpec((1, HEAD_DIM), const),
            tab, tab, tab,
        ],
        out_specs=[
            pl.BlockSpec((tm, nq), lambda i: (i, 0)),
            pl.BlockSpec((tm * N_KV_HEADS, HEAD_DIM), lambda i: (i, 0)),
            pl.BlockSpec((tm * N_KV_HEADS, HEAD_DIM), lambda i: (i, 0)),
        ],
        out_shape=[
            jax.ShapeDtypeStruct((n, nq), F32),
            jax.ShapeDtypeStruct((n * N_KV_HEADS, HEAD_DIM), F32),
            jax.ShapeDtypeStruct((n * N_KV_HEADS, HEAD_DIM), F32),
        ],
        compiler_params=_cparams(("parallel",)),
        name="qkv_sample" if precise else "qkv",
    )(h, n1, w, qn, kn, *tables)


def _attn_prompt_kernel(q_ref, k_ref, v_ref, h_ref, wo_ref, out_ref,
                        kbf, vtb, kmean, s_scr, p_scr, o_scr):
    blk = MOBA_BLOCK
    nblk = kbf.shape[0] // blk
    own = pl.program_id(1)
    scale = HEAD_DIM ** -0.5

    @pl.when(own == 0)
    def _():
        for n in range(nblk):
            rows = slice(n * blk, (n + 1) * blk)
            for kh in range(N_KV_HEADS):
                kcols = slice(kh * HEAD_DIM, (kh + 1) * HEAD_DIM)
                src_rows = pl.ds(n * blk * N_KV_HEADS + kh, blk, stride=N_KV_HEADS)
                kk = k_ref[src_rows, :]
                kbf[rows, kcols] = kk.astype(BF16)
                km_hi, km_lo = _split_bf16(jnp.mean(kk, axis=0, keepdims=True))
                kmean[n:n + 1, kcols] = km_hi.astype(F32)
                kmean[nblk + n:nblk + n + 1, kcols] = km_lo.astype(F32)
                vtb[kh, :HEAD_DIM, rows] = v_ref[src_rows, :].T.astype(BF16)
        for kh in range(N_KV_HEADS):
            vtb[kh, HEAD_DIM:, :] = jnp.ones((vtb.shape[1] - HEAD_DIM, vtb.shape[2]), BF16)

    cols = KV_GROUP * blk
    nidx = lax.broadcasted_iota(jnp.int32, (nblk, cols), 0)
    kq_diff = (lax.broadcasted_iota(jnp.int32, (blk, cols), 0)
               - lax.broadcasted_iota(jnp.int32, (blk, cols), 1) % blk)

    for kh in range(N_KV_HEADS):
        kcols = slice(kh * HEAD_DIM, (kh + 1) * HEAD_DIM)
        q2 = jnp.concatenate(
            [q_ref[:, (KV_GROUP * kh + i) * HEAD_DIM:(KV_GROUP * kh + i + 1) * HEAD_DIM]
             for i in range(KV_GROUP)], axis=0)
        q2s = (q2 * (scale * LOG2E)).astype(BF16)

        g2 = _dot_nt(kmean[:, kcols].astype(BF16), q2s)
        g_t = g2[:nblk] + g2[nblk:]
        rank = jnp.zeros_like(g_t)
        for m in range(nblk - 1):
            gm = g_t[m:m + 1, :]
            beats = (gm > g_t) | ((gm == g_t) & (nidx > m))
            inc = jnp.where(m < own, 1.0, 0.0)
            rank = rank + jnp.where(beats, inc, 0.0)
        sel_t = jnp.where(nidx == own, 1.0,
                          jnp.where((nidx < own) & (rank < float(MOBA_TOP_K)), 1.0, 0.0))

        def attend(nb, kh=kh, kcols=kcols, q2s=q2s, sel_t=sel_t):
            m = jnp.full((1, cols), NEG_INF, F32)
            for n in range(nb):
                rows = slice(n * blk, (n + 1) * blk)
                s = _dot_nt(kbf[rows, kcols], q2s)
                s = jnp.where(sel_t[n:n + 1, :] > 0.5, s, NEG_INF)
                if n >= nb - SPAN_BLOCKS:
                    s = jnp.where(kq_diff <= (own - n) * blk, s, NEG_INF)
                s_scr[rows, :] = s
                m = jnp.maximum(m, jnp.max(s, axis=0, keepdims=True))
            for n in range(nb):
                rows = slice(n * blk, (n + 1) * blk)
                p_scr[rows, :] = jnp.exp2(s_scr[rows, :] - m).astype(BF16)
            acc = _dot(vtb[kh, :, :nb * blk], p_scr[:nb * blk, :])
            o = (acc[:HEAD_DIM] / acc[HEAD_DIM:HEAD_DIM + 1]).T.astype(BF16)
            for i in range(KV_GROUP):
                hd = KV_GROUP * kh + i
                o_scr[:, hd * HEAD_DIM:(hd + 1) * HEAD_DIM] = o[i * blk:(i + 1) * blk]

        for nb in range(SPAN_BLOCKS, nblk + SPAN_BLOCKS, SPAN_BLOCKS):
            pl.when((own >= nb - SPAN_BLOCKS) & (own < nb))(functools.partial(attend, min(nb, nblk)))

    out_ref[...] = h_ref[...] + _dot(o_scr[...], wo_ref[...])


def _attn_prompt(q, k, v, h, wo, layer, *, batch, seq):
    blk = MOBA_BLOCK
    nqb = seq // blk
    d = h.shape[1]
    nq = q.shape[1]
    nk = N_KV_HEADS * HEAD_DIM
    cols = KV_GROUP * blk
    return pl.pallas_call(
        _attn_prompt_kernel,
        grid=(batch, nqb),
        in_specs=[
            pl.BlockSpec((blk, nq), lambda b, i: (b * nqb + i, 0)),
            pl.BlockSpec((seq * N_KV_HEADS, HEAD_DIM), lambda b, i: (b, 0)),
            pl.BlockSpec((seq * N_KV_HEADS, HEAD_DIM), lambda b, i: (b, 0)),
            pl.BlockSpec((blk, d), lambda b, i: (b * nqb + i, 0)),
            _layer_spec((nq, d), layer),
        ],
        out_specs=pl.BlockSpec((blk, d), lambda b, i: (b * nqb + i, 0)),
        out_shape=jax.ShapeDtypeStruct(h.shape, F32),
        scratch_shapes=[
            pltpu.VMEM((seq, nk), BF16),
            pltpu.VMEM((N_KV_HEADS, HEAD_DIM + 2 * SUBLANES, seq), BF16),
            pltpu.VMEM((2 * nqb, nk), F32),
            pltpu.VMEM((seq, cols), F32),
            pltpu.VMEM((seq, cols), BF16),
            pltpu.VMEM((blk, nq), BF16),
        ],
        compiler_params=_cparams(("parallel", "arbitrary")),
        name="attn_prompt",
    )(q, k, v, h, wo)


def _attn_sample_kernel(pt_ref, qd_ref, knew_ref, vnew_ref, ck_hbm, cv_hbm, out_ref,
                        kbuf, vbuf, s_scr, p_scr, kmean_scr, ksem, vsem,
                        *, base, npg, dec_seq):
    b = pl.program_id(0)
    nbatch = pl.num_programs(0)
    n_pages, prow, _ = kbuf.shape
    nchunk = n_pages // npg
    ccols = npg * prow
    bcols = MOBA_BLOCK * N_KV_HEADS
    pages_per_blk = bcols // prow
    blk_per_chunk = npg // pages_per_blk
    nrow = qd_ref.shape[0]
    nl = kmean_scr.shape[0]
    nblk = nl // SUBLANES
    scale = HEAD_DIM ** -0.5

    def page_copy(hbm, buf, sem, bb, pg):
        return pltpu.make_async_copy(hbm.at[base + pt_ref[bb, pg]], buf.at[pg], sem.at[pg // npg])

    def fetch(hbm, buf, sem, bb):
        for pg in range(n_pages):
            page_copy(hbm, buf, sem, bb, pg).start()

    def wait_chunk(hbm, buf, sem, c):
        for pg in range(c * npg, (c + 1) * npg):
            page_copy(hbm, buf, sem, b, pg).wait()

    @pl.when(b == 0)
    def _():
        fetch(ck_hbm, kbuf, ksem, 0)
        fetch(cv_hbm, vbuf, vsem, 0)

    qd = qd_ref[...]
    q_hi, q_lo = _split_bf16(qd)
    q_hl = jnp.concatenate([q_hi, q_lo], axis=0)

    for c in range(nchunk):
        wait_chunk(ck_hbm, kbuf, ksem, c)
        kf = [kbuf[c * npg + i] for i in range(npg)]
        for i in range(blk_per_chunk):
            tot = kf[pages_per_blk * i]
            for j in range(1, pages_per_blk):
                tot = tot + kf[pages_per_blk * i + j]
            part = jnp.sum(tot.reshape(prow // SUBLANES, SUBLANES, HEAD_DIM), axis=0)
            part = (part + pltpu.roll(part, N_KV_HEADS, 0)) * (1.0 / MOBA_BLOCK)
            row0 = (c * blk_per_chunk + i) * SUBLANES
            kmean_scr[row0:row0 + SUBLANES, :] = part
        k_hi, k_lo = _split_bf16(jnp.concatenate(kf, axis=0))
        s2 = _dot_nt(q_hl, k_hi)
        s_scr[:, c * ccols:(c + 1) * ccols] = (s2[:nrow] + s2[nrow:]) + _dot_nt(q_hi, k_lo)

    @pl.when(b + 1 < nbatch)
    def _():
        fetch(ck_hbm, kbuf, ksem, b + 1)

    g = _dot3(qd, kmean_scr[...], nt=True)
    lane = lax.broadcasted_iota(jnp.int32, (nrow, nl), 1)
    kv_r = (lax.broadcasted_iota(jnp.int32, (nrow, nl), 0) // dec_seq) // KV_GROUP
    rowsel = [jnp.where(kv_r == kh, 1.0, 0.0) for kh in range(N_KV_HEADS)]
    rank = jnp.zeros_like(g)
    for m in range(nblk):
        for kh in range(N_KV_HEADS):
            pos = m * SUBLANES + kh
            col = g[:, pos:pos + 1]
            beats = (col > g) | ((col == g) & (lane > pos))
            rank = rank + jnp.where(beats, rowsel[kh], 0.0)
    sel = jnp.where((lane % SUBLANES) == kv_r,
                    jnp.where(rank < float(MOBA_TOP_K), 1.0, 0.0), 0.0)
    blkid = lane // SUBLANES

    kv_c = (lax.broadcasted_iota(jnp.int32, (nrow, 1), 0) // dec_seq) // KV_GROUP
    t_row = lax.broadcasted_iota(jnp.int32, (nrow, 1), 0) % dec_seq

    def rows_for(x, t):
        out = jnp.zeros((nrow, HEAD_DIM), F32)
        for kh in range(N_KV_HEADS):
            r = t * N_KV_HEADS + kh
            out = out + jnp.where(kv_c == kh, x[r:r + 1, :], 0.0)
        return out

    knew = knew_ref[...]
    vnew = vnew_ref[...]
    s_own = [jnp.sum(qd * rows_for(knew, t), axis=-1, keepdims=True) * scale
             for t in range(dec_seq)]
    ok_own = [t_row >= t for t in range(dec_seq)]

    kh_match = (lax.broadcasted_iota(jnp.int32, (nrow, bcols), 1) % N_KV_HEADS) == (
        (lax.broadcasted_iota(jnp.int32, (nrow, bcols), 0) // dec_seq) // KV_GROUP)
    m = jnp.full((nrow, 1), NEG_INF, F32)
    for t in range(dec_seq):
        m = jnp.maximum(m, jnp.where(ok_own[t], s_own[t], NEG_INF))
    for n in range(nblk):
        cs = slice(n * bcols, (n + 1) * bcols)
        pick = jnp.max(jnp.where(blkid == n, sel, 0.0), axis=1, keepdims=True)
        sc = jnp.where(pick > 0.5, s_scr[:, cs] * scale, NEG_INF)
        sc = jnp.where(kh_match, sc, NEG_INF)
        s_scr[:, cs] = sc
        m = jnp.maximum(m, jnp.max(sc, axis=-1, keepdims=True))
    l = jnp.zeros((nrow, 1), F32)
    for n in range(nblk):
        cs = slice(n * bcols, (n + 1) * bcols)
        p = jnp.exp(s_scr[:, cs] - m)
        l = l + jnp.sum(p, axis=-1, keepdims=True)
        p_hi, p_lo = _split_bf16(p)
        p_scr[:nrow, cs] = p_hi
        p_scr[nrow:, cs] = p_lo
    acc = jnp.zeros((nrow, HEAD_DIM), F32)
    for t in range(dec_seq):
        p_t = jnp.where(ok_own[t], jnp.exp(s_own[t] - m), 0.0)
        l = l + p_t
        acc = acc + p_t * rows_for(vnew, t)

    for c in range(nchunk):
        wait_chunk(cv_hbm, vbuf, vsem, c)
        vb = jnp.concatenate([vbuf[c * npg + i].astype(BF16) for i in range(npg)], axis=0)
        pv = _dot(p_scr[:, c * ccols:(c + 1) * ccols], vb)
        acc = acc + (pv[:nrow] + pv[nrow:])

    @pl.when(b + 1 < nbatch)
    def _():
        fetch(cv_hbm, vbuf, vsem, b + 1)

    out_ref[...] = acc / l


def _attn_sample(q, k_new, v_new, cache_k, cache_v, page_table, layer, *, n_pool, batch, dec_seq,
                 npg=8):
    n_pages = page_table.shape[1]
    prow = cache_k.shape[1]
    page = prow // N_KV_HEADS
    assert (n_pages * page) % MOBA_BLOCK == 0 and dec_seq <= MOBA_BLOCK
    assert n_pages % npg == 0 and MOBA_BLOCK % page == 0 and npg % (MOBA_BLOCK // page) == 0
    assert SUBLANES == 2 * N_KV_HEADS and prow % SUBLANES == 0
    nchunk = n_pages // npg
    nblk = n_pages * page // MOBA_BLOCK
    nrow = N_HEADS * dec_seq
    qd = q.reshape(batch, dec_seq, N_HEADS, HEAD_DIM).transpose(0, 2, 1, 3)
    qd = qd.reshape(batch, nrow, HEAD_DIM)
    per_b = lambda b, pt: (b, 0, 0)
    new_rows = dec_seq * N_KV_HEADS
    grid_spec = pltpu.PrefetchScalarGridSpec(
        num_scalar_prefetch=1,
        grid=(batch,),
        in_specs=[
            pl.BlockSpec((None, nrow, HEAD_DIM), per_b),
            pl.BlockSpec((None, new_rows, HEAD_DIM), per_b),
            pl.BlockSpec((None, new_rows, HEAD_DIM), per_b),
            pl.BlockSpec(memory_space=pl.ANY),
            pl.BlockSpec(memory_space=pl.ANY),
        ],
        out_specs=pl.BlockSpec((None, nrow, HEAD_DIM), per_b),
        scratch_shapes=[
            pltpu.VMEM((n_pages, prow, HEAD_DIM), F32),
            pltpu.VMEM((n_pages, prow, HEAD_DIM), F32),
            pltpu.VMEM((nrow, n_pages * prow), F32),
            pltpu.VMEM((2 * nrow, n_pages * prow), BF16),
            pltpu.VMEM((nblk * SUBLANES, HEAD_DIM), F32),
            pltpu.SemaphoreType.DMA((nchunk,)),
            pltpu.SemaphoreType.DMA((nchunk,)),
        ],
    )
    o = pl.pallas_call(
        functools.partial(_attn_sample_kernel, base=layer * n_pool, npg=npg, dec_seq=dec_seq),
        grid_spec=grid_spec,
        out_shape=jax.ShapeDtypeStruct((batch, nrow, HEAD_DIM), F32),
        compiler_params=_cparams(("arbitrary",)),
        name="attn_sample",
    )(page_table, qd, k_new.reshape(batch, new_rows, HEAD_DIM),
      v_new.reshape(batch, new_rows, HEAD_DIM), cache_k, cache_v)
    o = o.reshape(batch, N_HEADS, dec_seq, HEAD_DIM).transpose(0, 2, 1, 3)
    return o.reshape(batch * dec_seq, N_HEADS * HEAD_DIM)


def _oproj_kernel(o_ref, h_ref, wo_ref, out_ref):
    out_ref[...] = h_ref[...] + _dot3(o_ref[...], wo_ref[...])


def _oproj(o, h, wo, layer):
    n, d = h.shape
    full = pl.BlockSpec((n, d), lambda i: (0, 0))
    return pl.pallas_call(
        _oproj_kernel,
        grid=(1,),
        in_specs=[pl.BlockSpec(o.shape, lambda i: (0, 0)), full,
                  _layer_spec(wo.shape[1:], layer)],
        out_specs=full,
        out_shape=jax.ShapeDtypeStruct(h.shape, F32),
        compiler_params=_cparams(("arbitrary",)),
        name="oproj_sample",
    )(o, h, wo)


def _moe_kernel(h_ref, n2_ref, wr_ref, br_ref, wg_ref, wu_ref, wd_ref, out_ref,
                xn_scr, gates_scr, acc_scr, *, precise):
    g = pl.program_id(1)
    ngroups = pl.num_programs(1)
    tm = h_ref.shape[0]
    lane = lax.broadcasted_iota(jnp.int32, (tm, LANES), 1)
    first_expert_lane = N_EXPERT_GROUPS

    @pl.when(g == 0)
    def _():
        x = h_ref[...]
        xn = _rms(x, n2_ref[...])
        xn_scr[...] = xn.astype(xn_scr.dtype)
        acc_scr[...] = x
        _, gates_scr[...] = _route(_dot3(xn, wr_ref[...]) + br_ref[...], lane)

    xb = xn_scr[...]
    gates = gates_scr[...]
    acc = acc_scr[...]
    for e in range(EXPERTS_PER_GROUP):
        a = _mm(xb, wg_ref[e], precise)
        b = _mm(xb, wu_ref[e], precise)
        tgt = first_expert_lane + g * EXPERTS_PER_GROUP + e
        gate = jnp.sum(jnp.where(lane == tgt, gates, 0.0), axis=-1, keepdims=True)
        hid = _silu(a) * b * gate
        acc = acc + _mm(hid, wd_ref[e], precise)
    acc_scr[...] = acc

    @pl.when(g == ngroups - 1)
    def _():
        out_ref[...] = acc_scr[...]


def _moe(h, n2, wr, br, wg, wu, wd, layer, *, tm, precise):
    n, d = h.shape
    de = wg.shape[3]
    epg = EXPERTS_PER_GROUP
    const = lambda i, g: (0, 0)
    group = lambda i, g: (layer, g, 0, 0)
    return pl.pallas_call(
        functools.partial(_moe_kernel, precise=precise),
        grid=(n // tm, N_EXPERT_GROUPS),
        in_specs=[
            pl.BlockSpec((tm, d), lambda i, g: (i, 0)),
            pl.BlockSpec((1, d), const),
            pl.BlockSpec((d, LANES), const),
            pl.BlockSpec((1, LANES), const),
            pl.BlockSpec((None, epg, d, de), group),
            pl.BlockSpec((None, epg, d, de), group),
            pl.BlockSpec((None, epg, de, d), group),
        ],
        out_specs=pl.BlockSpec((tm, d), lambda i, g: (i, 0)),
        out_shape=jax.ShapeDtypeStruct((n, d), F32),
        scratch_shapes=[
            pltpu.VMEM((tm, d), F32 if precise else BF16),
            pltpu.VMEM((tm, LANES), F32),
            pltpu.VMEM((tm, d), F32),
        ],
        compiler_params=_cparams(("parallel", "arbitrary")),
        name="moe_sample" if precise else "moe",
    )(h, n2, wr, br, wg, wu, wd)


def _route(logits, lane):
    first_expert_lane = N_EXPERT_GROUPS
    is_group = lane < N_EXPERT_GROUPS
    gl = jnp.where(is_group, logits, NEG_INF)
    ge = jnp.where(is_group, jnp.exp(gl - jnp.max(gl, axis=-1, keepdims=True)), 0.0)
    gprob = ge / jnp.sum(ge, axis=-1, keepdims=True)
    g_w = jnp.max(gprob, axis=-1, keepdims=True)
    g_idx = jnp.min(jnp.where(is_group & (gprob == g_w), lane, LANES), axis=-1, keepdims=True)
    lo = first_expert_lane + g_idx * EXPERTS_PER_GROUP
    in_group = (lane >= lo) & (lane < lo + EXPERTS_PER_GROUP)
    el = jnp.where(in_group, logits, NEG_INF)
    ee = jnp.where(in_group, jnp.exp(el - jnp.max(el, axis=-1, keepdims=True)), 0.0)
    eprob = jnp.where(in_group, ee / jnp.sum(ee, axis=-1, keepdims=True), -1.0)
    p1 = jnp.max(eprob, axis=-1, keepdims=True)
    i1 = jnp.min(jnp.where(eprob == p1, lane, LANES), axis=-1, keepdims=True)
    rest = jnp.where(lane == i1, -1.0, eprob)
    p2 = jnp.max(rest, axis=-1, keepdims=True)
    i2 = jnp.min(jnp.where(rest == p2, lane, LANES), axis=-1, keepdims=True)
    denom = p1 + p2
    w1 = p1 / denom * g_w
    w2 = p2 / denom * g_w
    return g_idx, jnp.where(lane == i1, w1, 0.0) + jnp.where(lane == i2, w2, 0.0)


def _route_kernel(h_ref, n2_ref, wr_ref, br_ref, hx_ref, meta_ref, cnt_ref, run_scr, ltri_scr):
    i = pl.program_id(0)
    tm, d = h_ref.shape

    @pl.when(i == 0)
    def _():
        run_scr[...] = jnp.zeros_like(run_scr)
        r = lax.broadcasted_iota(jnp.int32, (tm, tm), 0)
        c = lax.broadcasted_iota(jnp.int32, (tm, tm), 1)
        ltri_scr[...] = jnp.where(c < r, 1.0, 0.0).astype(BF16)

    lane = lax.broadcasted_iota(jnp.int32, (tm, LANES), 1)
    x = h_ref[...]
    xn = _rms(x, n2_ref[...])
    g_idx, gates = _route(_dot3(xn, wr_ref[...]) + br_ref[...], lane)
    onehot = jnp.where(lane == g_idx, 1.0, 0.0)
    before = _dot(ltri_scr[...], onehot.astype(BF16)) + run_scr[...]
    rank = jnp.sum(onehot * before, axis=-1, keepdims=True)
    run_scr[...] += jnp.sum(onehot, axis=0, keepdims=True)
    info = (gates + jnp.where(lane == 0, g_idx.astype(F32), 0.0)
            + jnp.where(lane == 1, rank, 0.0))
    hx_ref[:, :d] = x
    hx_ref[:, d:] = info
    rr = lax.broadcasted_iota(jnp.int32, (SUBLANES, LANES), 0)
    ll = lax.broadcasted_iota(jnp.int32, (SUBLANES, LANES), 1)
    pick = jnp.where((rr == ll) & (rr < 2), 1.0, 0.0).astype(BF16)
    ih, il = _split_bf16(info)
    meta_ref[...] = _dot_nt(pick, ih) + _dot_nt(pick, il)
    cnt_ref[...] = jnp.broadcast_to(run_scr[...], cnt_ref.shape)


def _route_call(h, n2, wr, br, *, tm):
    n, d = h.shape
    const = lambda i: (0, 0)
    return pl.pallas_call(
        _route_kernel,
        grid=(n // tm,),
        in_specs=[
            pl.BlockSpec((tm, d), lambda i: (i, 0)),
            pl.BlockSpec((1, d), const),
            pl.BlockSpec((d, LANES), const),
            pl.BlockSpec((1, LANES), const),
        ],
        out_specs=[
            pl.BlockSpec((tm, d + LANES), lambda i: (i, 0)),
            pl.BlockSpec((SUBLANES, tm), lambda i: (0, i)),
            pl.BlockSpec((SUBLANES, LANES), const),
        ],
        out_shape=[
            jax.ShapeDtypeStruct((n, d + LANES), F32),
            jax.ShapeDtypeStruct((SUBLANES, n), F32),
            jax.ShapeDtypeStruct((SUBLANES, LANES), F32),
        ],
        scratch_shapes=[pltpu.VMEM((1, LANES), F32), pltpu.VMEM((tm, tm), BF16)],
        compiler_params=_cparams(("arbitrary",)),
        name="moe_route",
    )(h, n2, wr, br)


def _permute_kernel(idx_ref, pad_start_ref, pad_cnt_ref, src, zero, dst, sem, pad_sem,
                    *, n, chunk, scatter):
    nchunks = n // chunk

    def row_copy(i, slot):
        j = idx_ref[i]
        s, t = (i, j) if scatter else (j, i)
        return pltpu.make_async_copy(src.at[pl.ds(s, 1)], dst.at[pl.ds(t, 1)], sem.at[slot])

    def issue(c):
        def body(k, carry):
            row_copy(c * chunk + k, c % 2).start()
            return carry
        lax.fori_loop(0, chunk, body, 0, unroll=8)

    def drain(c):
        pltpu.make_async_copy(src.at[pl.ds(0, chunk)], dst.at[pl.ds(0, chunk)],
                              sem.at[c % 2]).wait()

    for c in range(nchunks):
        issue(c)
        if c >= 1:
            drain(c - 1)
    drain(nchunks - 1)

    if scatter:
        def pad_copy(g, k):
            return pltpu.make_async_copy(zero.at[pl.ds(0, 1)],
                                         dst.at[pl.ds(pad_start_ref[g] + k, 1)], pad_sem)

        for g in range(N_EXPERT_GROUPS):
            def start_body(k, carry, g=g):
                pad_copy(g, k).start()
                return carry

            def wait_body(k, carry, g=g):
                pad_copy(g, k).wait()
                return carry

            lax.fori_loop(0, pad_cnt_ref[g], start_body, 0)
            lax.fori_loop(0, pad_cnt_ref[g], wait_body, 0)

        tile = zero.shape[0]
        n_used = pad_cnt_ref[N_EXPERT_GROUPS]

        def tile_copy(t):
            start = pl.multiple_of(t * tile, tile)
            return pltpu.make_async_copy(zero, dst.at[pl.ds(start, tile)], pad_sem)

        def tile_body(t, carry):
            tile_copy(t).start()
            tile_copy(t).wait()
            return carry

        lax.fori_loop(n_used, dst.shape[0] // tile, tile_body, 0)


def _permute(src, idx, n_out, *, scatter, tile, pad_start=None, pad_cnt=None, chunk=1024):
    n = idx.shape[0]
    w = src.shape[1]
    if pad_start is None:
        pad_start = jnp.zeros((N_EXPERT_GROUPS,), jnp.int32)
        pad_cnt = jnp.zeros((N_EXPERT_GROUPS + 1,), jnp.int32)
    zero = jnp.zeros((tile if scatter else SUBLANES, w), F32)
    anyspec = pl.BlockSpec(memory_space=pl.ANY)
    grid_spec = pltpu.PrefetchScalarGridSpec(
        num_scalar_prefetch=3,
        grid=(1,),
        in_specs=[anyspec, anyspec],
        out_specs=anyspec,
        scratch_shapes=[pltpu.SemaphoreType.DMA((2,)), pltpu.SemaphoreType.DMA(())],
    )
    return pl.pallas_call(
        functools.partial(_permute_kernel, n=n, chunk=chunk, scatter=scatter),
        grid_spec=grid_spec,
        out_shape=jax.ShapeDtypeStruct((n_out, w), F32),
        compiler_params=pltpu.CompilerParams(dimension_semantics=("arbitrary",)),
        name="moe_dispatch" if scatter else "moe_combine",
    )(idx, pad_start, pad_cnt, src, zero)


def _moe_sorted_kernel(tg_ref, nu_ref, xs_ref, n2_ref, wg_ref, wu_ref, wd_ref, ys_ref):
    t = pl.program_id(0)

    @pl.when(t < nu_ref[0])
    def _():
        tm, d = ys_ref.shape
        x = xs_ref[:, :d]
        info = xs_ref[:, d:]
        xb = _rms(x, n2_ref[...]).astype(BF16)
        lane = lax.broadcasted_iota(jnp.int32, (tm, LANES), 1)
        first = N_EXPERT_GROUPS + tg_ref[t] * EXPERTS_PER_GROUP
        acc = x
        for e in range(EXPERTS_PER_GROUP):
            a = _dot(xb, wg_ref[e])
            b = _dot(xb, wu_ref[e])
            gate = jnp.sum(jnp.where(lane == first + e, info, 0.0), axis=-1, keepdims=True)
            hid = _silu(a) * b * gate
            acc = acc + _dot(hid.astype(BF16), wd_ref[e])
        ys_ref[...] = acc

    @pl.when(t >= nu_ref[0])
    def _():
        ys_ref[...] = jnp.zeros_like(ys_ref)


def _moe_sorted(xs, tile_group, n_used, n2, wg, wu, wd, layer, *, tm):
    s, wx = xs.shape
    d = wx - LANES
    de = wg.shape[3]
    epg = EXPERTS_PER_GROUP
    last = lambda t, tg, nu: (jnp.minimum(t, nu[0] - 1), 0)
    wmap = lambda t, tg, nu: (layer, tg[t], 0, 0)
    grid_spec = pltpu.PrefetchScalarGridSpec(
        num_scalar_prefetch=2,
        grid=(s // tm,),
        in_specs=[
            pl.BlockSpec((tm, wx), last),
            pl.BlockSpec((1, d), lambda t, tg, nu: (0, 0)),
            pl.BlockSpec((None, epg, d, de), wmap),
            pl.BlockSpec((None, epg, d, de), wmap),
            pl.BlockSpec((None, epg, de, d), wmap),
        ],
        out_specs=pl.BlockSpec((tm, d), lambda t, tg, nu: (t, 0)),
    )
    return pl.pallas_call(
        _moe_sorted_kernel,
        grid_spec=grid_spec,
        out_shape=jax.ShapeDtypeStruct((s, d), F32),
        compiler_params=_cparams(("arbitrary",)),
        name="moe_sorted",
    )(tile_group, n_used, xs, n2, wg, wu, wd)


def _moe_prompt(h, n2, wr, br, wg, wu, wd, layer, *, tm):
    n, d = h.shape
    ng = N_EXPERT_GROUPS
    t_max = n // tm + ng
    hx, meta, cnt = _route_call(h, n2, wr, br, tm=tm)
    gidx = meta[0].astype(jnp.int32)
    rank = meta[1].astype(jnp.int32)
    counts = cnt[0, :ng].astype(jnp.int32)
    tiles = (counts + tm - 1) // tm
    ends = jnp.cumsum(tiles)
    starts = ends - tiles
    dest = starts[gidx] * tm + rank
    n_used = ends[ng - 1:ng]
    t_ids = jnp.minimum(jnp.arange(t_max, dtype=jnp.int32), n_used - 1)
    tile_group = jnp.sum((t_ids[:, None] >= ends[None, :]).astype(jnp.int32), axis=1)
    xs = _permute(hx, dest, t_max * tm, scatter=True, tile=tm,
                  pad_start=starts * tm + counts,
                  pad_cnt=jnp.concatenate([tiles * tm - counts, n_used]))
    ys = _moe_sorted(xs, tile_group, n_used, n2, wg, wu, wd, layer, tm=tm)
    return _permute(ys, dest, n, scatter=False, tile=tm)


def kernel(x_prompt, x_sample, cache_k, cache_v, page_table, norm1, norm2, a_w_in, a_g_v, a_w_s, a_b_s, a_w_out, b_w_qkv, b_q_norm, b_k_norm, b_w_o, moe_w_group, moe_b_group, moe_w_router, moe_b_router, moe_w_gate, moe_w_up, moe_w_down):
    bp, tp, d = x_prompt.shape
    bs, ts, _ = x_sample.shape
    depth = norm1.shape[0]
    n_attn, n_pool, page = cache_k.shape[0], cache_k.shape[1], cache_k.shape[2]
    past_len = page_table.shape[1] * page
    assert CHUNK % ts == 0 and (bs * ts) % CHUNK == 0 and tp % MOBA_BLOCK == 0

    hp = x_prompt.reshape(bp * tp, d)
    hs = x_sample.reshape(bs * ts, d)
    ck = cache_k.reshape(n_attn * n_pool, page * N_KV_HEADS, HEAD_DIM)
    cv = cache_v.reshape(n_attn * n_pool, page * N_KV_HEADS, HEAD_DIM)
    tab_p = _rope_tables(jnp.arange(tp, dtype=jnp.int32))
    pos_s = past_len + jnp.arange(ts, dtype=jnp.int32)
    tab_s = tuple(jnp.tile(t, (bs, 1)) for t in _rope_tables(pos_s))

    pad = LANES - N_EXPERT_GROUPS - N_EXPERTS
    wr_all = jnp.concatenate(
        [moe_w_group, moe_w_router, jnp.zeros((depth, d, pad), F32)], axis=2)
    br_all = jnp.concatenate(
        [moe_b_group, moe_b_router, jnp.zeros((depth, pad), F32)], axis=1)[:, None, :]

    w_in_bf, w_out_bf = a_w_in.astype(BF16), a_w_out.astype(BF16)
    w_qkv_bf, w_o_bf = b_w_qkv.astype(BF16), b_w_o.astype(BF16)
    wg_bf, wu_bf, wd_bf = moe_w_gate.astype(BF16), moe_w_up.astype(BF16), moe_w_down.astype(BF16)

    new_k_p, new_v_p, new_k_s, new_v_s, new_chunk_v = [], [], [], [], []
    reps = CHUNK // ts
    for i in range(depth):
        j = i // 2
        n1 = norm1[i][None, :]
        if i % 2 == 0:
            g_v = a_g_v[j][None, :]
            hp = _mixer_a(hp, n1, w_in_bf, g_v, a_w_s[j], a_b_s[j][:, :, None], w_out_bf, j,
                          tm=512, seq_len=CHUNK, emit_v=False, precise=False)
            wmix_s = jnp.tile(a_w_s[j][:, :ts, :ts], (1, reps, reps))
            bs_s = jnp.tile(a_b_s[j][:, :ts], (1, reps))[:, :, None]
            hs, v_rows = _mixer_a(hs, n1, a_w_in, g_v, wmix_s, bs_s, a_w_out, j,
                                  tm=CHUNK, seq_len=ts, emit_v=True, precise=True)
            new_chunk_v.append(v_rows.reshape(bs, ts, -1))
        else:
            qn = b_q_norm[j][None, :]
            kn = b_k_norm[j][None, :]
            qp, kp, vp = _qkv(hp, n1, w_qkv_bf, j, qn, kn, tab_p, tm=512, precise=False)
            qs, ks_new, vs_new = _qkv(hs, n1, b_w_qkv, j, qn, kn, tab_s, tm=bs * ts, precise=True)
            hp = _attn_prompt(qp, kp, vp, hp, w_o_bf, j, batch=bp, seq=tp)
            o_s = _attn_sample(qs, ks_new, vs_new, ck, cv, page_table, j, n_pool=n_pool,
                               batch=bs, dec_seq=ts)
            hs = _oproj(o_s, hs, b_w_o, j)
            new_k_p.append(kp.reshape(bp, tp, N_KV_HEADS, HEAD_DIM))
            new_v_p.append(vp.reshape(bp, tp, N_KV_HEADS, HEAD_DIM))
            new_k_s.append(ks_new.reshape(bs, ts, N_KV_HEADS, HEAD_DIM))
            new_v_s.append(vs_new.reshape(bs, ts, N_KV_HEADS, HEAD_DIM))
        n2 = norm2[i][None, :]
        hp = _moe_prompt(hp, n2, wr_all[i], br_all[i], wg_bf, wu_bf, wd_bf, i, tm=512)
        hs = _moe(hs, n2, wr_all[i], br_all[i], moe_w_gate, moe_w_up, moe_w_down, i,
                  tm=bs * ts, precise=True)
    return (hp.reshape(bp, tp, d), hs.reshape(bs, ts, d),
            jnp.stack(new_k_p), jnp.stack(new_v_p), jnp.stack(new_k_s), jnp.stack(new_v_s),
            jnp.stack(new_chunk_v))
```

```python
import functools

import jax
import jax.numpy as jnp
from jax import lax
from jax.experimental import pallas as pl
from jax.experimental.pallas import tpu as pltpu

F32 = jnp.float32
BF16 = jnp.bfloat16

CHUNK = 128
CHUNK_GROUPS = 4
HEAD_DIM = 128
N_HEADS = 8
N_KV_HEADS = 4
KV_GROUP = N_HEADS // N_KV_HEADS
ROT_DIM = HEAD_DIM // 4
ROPE_THETA = 500000.0
MOBA_BLOCK = 256
MOBA_TOP_K = 3
N_EXPERT_GROUPS = 4
EXPERTS_PER_GROUP = 4
N_EXPERTS = N_EXPERT_GROUPS * EXPERTS_PER_GROUP
NORM_EPS = 1e-6
NEG_INF = -1e30
LOG2E = 1.4426950408889634
SPAN_BLOCKS = 2

SUBLANES = 8
LANES = 128
VMEM_LIMIT_BYTES = 56 * 1024 * 1024

_NT = (((1,), (1,)), ((), ()))


def _dot(a, b):
    return jnp.dot(a, b, preferred_element_type=F32)


def _dot_nt(a, b):
    return lax.dot_general(a, b, _NT, preferred_element_type=F32)


def _split_bf16(a):
    hi = a.astype(BF16)
    lo = (a - hi.astype(F32)).astype(BF16)
    return hi, lo


def _dot3(a, b, nt=False):
    d = _dot_nt if nt else _dot
    ah, al = _split_bf16(a)
    bh, bl = _split_bf16(b)
    return d(ah, bh) + (d(ah, bl) + d(al, bh))


def _mm(x, w, precise):
    if precise:
        return _dot3(x, w)
    return _dot(x.astype(BF16), w.astype(BF16))


def _rms(x, g):
    ms = jnp.mean(x * x, axis=-1, keepdims=True)
    return x * lax.rsqrt(ms + NORM_EPS) * g


def _gelu_tanh(x):
    c = 0.7978845608028654
    return 0.5 * x * (1.0 + jnp.tanh(c * (x + 0.044715 * (x * x * x))))


def _silu(x):
    return x * (1.0 / (1.0 + jnp.exp(-x)))


def _cparams(sem):
    return pltpu.CompilerParams(dimension_semantics=sem, vmem_limit_bytes=VMEM_LIMIT_BYTES)


def _layer_spec(shape, layer):
    zeros = (0,) * len(shape)
    return pl.BlockSpec((None,) + tuple(shape), lambda *_: (layer,) + zeros)


def _mixer_a_kernel(h_ref, n1_ref, win_ref, gv_ref, wmix_ref, bs_ref, wout_ref, *rest,
                    seq_len, emit_v, precise):
    if emit_v:
        out_ref, v_ref, us_scr = rest
    else:
        out_ref, us_scr = rest
    tm = h_ref.shape[0]
    width = gv_ref.shape[1]
    gdim = width // CHUNK_GROUPS
    x = h_ref[...]
    xn = _rms(x, n1_ref[...])
    z = _gelu_tanh(_mm(xn, win_ref[...], precise))
    u = z[:, :width]
    vn = _rms(z[:, width:], gv_ref[...])
    if emit_v:
        v_ref[...] = vn
    row = lax.broadcasted_iota(jnp.int32, (CHUNK, CHUNK), 0)
    col = lax.broadcasted_iota(jnp.int32, (CHUNK, CHUNK), 1)
    mask = col <= row
    if seq_len < CHUNK:
        mask = mask & ((row // seq_len) == (col // seq_len))
    if not precise:
        vn = vn.astype(BF16)
    for g in range(CHUNK_GROUPS):
        wm = jnp.where(mask, wmix_ref[g], 0.0)
        bias = bs_ref[g]
        for c in range(tm // CHUNK):
            rows = slice(c * CHUNK, (c + 1) * CHUNK)
            cols = slice(g * gdim, (g + 1) * gdim)
            s = _mm(wm, vn[rows, cols], precise) + bias
            us_scr[rows, cols] = (u[rows, cols] * s).astype(us_scr.dtype)
    out_ref[...] = x + _mm(us_scr[...], wout_ref[...], precise)


def _mixer_a(h, n1, w_in, g_v, wmix, bs, w_out, layer, *, tm, seq_len, emit_v, precise):
    n, d = h.shape
    width = g_v.shape[1]
    const = lambda i: (0, 0)
    out_shape = [jax.ShapeDtypeStruct((n, d), F32)]
    out_specs = [pl.BlockSpec((tm, d), lambda i: (i, 0))]
    if emit_v:
        out_shape.append(jax.ShapeDtypeStruct((n, width), F32))
        out_specs.append(pl.BlockSpec((tm, width), lambda i: (i, 0)))
    res = pl.pallas_call(
        functools.partial(_mixer_a_kernel, seq_len=seq_len, emit_v=emit_v, precise=precise),
        grid=(n // tm,),
        in_specs=[
            pl.BlockSpec((tm, d), lambda i: (i, 0)),
            pl.BlockSpec((1, d), const),
            _layer_spec((d, 2 * width), layer),
            pl.BlockSpec((1, width), const),
            pl.BlockSpec((CHUNK_GROUPS, CHUNK, CHUNK), lambda i: (0, 0, 0)),
            pl.BlockSpec((CHUNK_GROUPS, CHUNK, 1), lambda i: (0, 0, 0)),
            _layer_spec((width, d), layer),
        ],
        out_specs=out_specs,
        out_shape=out_shape,
        scratch_shapes=[pltpu.VMEM((tm, width), F32 if precise else BF16)],
        compiler_params=_cparams(("parallel",)),
        name="mixer_a_sample" if emit_v else "mixer_a",
    )(h, n1, w_in, g_v, wmix, bs, w_out)
    return res if emit_v else res[0]


def _qkv_kernel(h_ref, n1_ref, w_ref, qn_ref, kn_ref, cos_ref, sa_ref, sb_ref,
                q_ref, k_ref, v_ref, *, precise):
    xn = _rms(h_ref[...], n1_ref[...])
    qkv = _mm(xn, w_ref[...], precise)
    cos = cos_ref[...]
    sa = sa_ref[...]
    sb = sb_ref[...]
    nq = q_ref.shape[1]
    nk = N_KV_HEADS * HEAD_DIM

    def head(t, g):
        t = _rms(t, g)
        return (t * cos + pltpu.roll(t, HEAD_DIM - ROT_DIM // 2, 1) * sa
                + pltpu.roll(t, ROT_DIM // 2, 1) * sb)

    tm = h_ref.shape[0]
    for hd in range(nq // HEAD_DIM):
        cols = slice(hd * HEAD_DIM, (hd + 1) * HEAD_DIM)
        q_ref[:, cols] = head(qkv[:, cols], qn_ref[...])
    for hd in range(N_KV_HEADS):
        rows = pl.ds(hd, tm, stride=N_KV_HEADS)
        k_ref[rows, :] = head(qkv[:, nq + hd * HEAD_DIM: nq + (hd + 1) * HEAD_DIM], kn_ref[...])
        v_ref[rows, :] = qkv[:, nq + nk + hd * HEAD_DIM: nq + nk + (hd + 1) * HEAD_DIM]


def _rope_tables(pos):
    half = ROT_DIM // 2
    inv_freq = jnp.power(ROPE_THETA, -jnp.arange(half, dtype=F32) * 2.0 / ROT_DIM)
    ang = pos.astype(F32)[:, None] * inv_freq[None, :]
    cos = jnp.cos(ang)
    sin = jnp.sin(ang)
    t = pos.shape[0]
    ones = jnp.ones((t, HEAD_DIM - ROT_DIM), F32)
    zeros = jnp.zeros((t, HEAD_DIM - ROT_DIM), F32)
    z16 = jnp.zeros((t, half), F32)
    cos_t = jnp.concatenate([cos, cos, ones], axis=1)
    sa_t = jnp.concatenate([-sin, z16, zeros], axis=1)
    sb_t = jnp.concatenate([z16, sin, zeros], axis=1)
    return cos_t, sa_t, sb_t


def _qkv(h, n1, w, layer, qn, kn, tables, *, tm, precise):
    n, d = h.shape
    nq = N_HEADS * HEAD_DIM
    nk = N_KV_HEADS * HEAD_DIM
    t = tables[0].shape[0]
    tiles_per_seq = t // tm
    const = lambda i: (0, 0)
    tab = pl.BlockSpec((tm, HEAD_DIM), lambda i: (i % tiles_per_seq, 0))
    return pl.pallas_call(
        functools.partial(_qkv_kernel, precise=precise),
        grid=(n // tm,),
        in_specs=[
            pl.BlockSpec((tm, d), lambda i: (i, 0)),
            pl.BlockSpec((1, d), const),
            _layer_spec((d, nq + 2 * nk), layer),
            pl.BlockSpec((1, HEAD_DIM), const),
            pl.BlockSpec((1, HEAD_DIM), const),
            tab, tab, tab,
        ],
        out_specs=[
            pl.BlockSpec((tm, nq), lambda i: (i, 0)),
            pl.BlockSpec((tm * N_KV_HEADS, HEAD_DIM), lambda i: (i, 0)),
            pl.BlockSpec((tm * N_KV_HEADS, HEAD_DIM), lambda i: (i, 0)),
        ],
        out_shape=[
            jax.ShapeDtypeStruct((n, nq), F32),
            jax.ShapeDtypeStruct((n * N_KV_HEADS, HEAD_DIM), F32),
            jax.ShapeDtypeStruct((n * N_KV_HEADS, HEAD_DIM), F32),
        ],
        compiler_params=_cparams(("parallel",)),
        name="qkv_sample" if precise else "qkv",
    )(h, n1, w, qn, kn, *tables)


def _attn_prompt_kernel(q_ref, k_ref, v_ref, h_ref, wo_ref, out_ref,
                        kbf, vtb, kmean, s_scr, p_scr, o_scr):
    blk = MOBA_BLOCK
    nblk = kbf.shape[0] // blk
    own = pl.program_id(1)
    scale = HEAD_DIM ** -0.5

    @pl.when(own == 0)
    def _():
        for n in range(nblk):
            rows = slice(n * blk, (n + 1) * blk)
            for kh in range(N_KV_HEADS):
                kcols = slice(kh * HEAD_DIM, (kh + 1) * HEAD_DIM)
                src_rows = pl.ds(n * blk * N_KV_HEADS + kh, blk, stride=N_KV_HEADS)
                kk = k_ref[src_rows, :]
                kbf[rows, kcols] = kk.astype(BF16)
                km_hi, km_lo = _split_bf16(jnp.mean(kk, axis=0, keepdims=True))
                kmean[n:n + 1, kcols] = km_hi.astype(F32)
                kmean[nblk + n:nblk + n + 1, kcols] = km_lo.astype(F32)
                vtb[kh, :HEAD_DIM, rows] = v_ref[src_rows, :].T.astype(BF16)
        for kh in range(N_KV_HEADS):
            vtb[kh, HEAD_DIM:, :] = jnp.ones((vtb.shape[1] - HEAD_DIM, vtb.shape[2]), BF16)

    cols = KV_GROUP * blk
    nidx = lax.broadcasted_iota(jnp.int32, (nblk, cols), 0)
    kq_diff = (lax.broadcasted_iota(jnp.int32, (blk, cols), 0)
               - lax.broadcasted_iota(jnp.int32, (blk, cols), 1) % blk)

    for kh in range(N_KV_HEADS):
        kcols = slice(kh * HEAD_DIM, (kh + 1) * HEAD_DIM)
        q2 = jnp.concatenate(
            [q_ref[:, (KV_GROUP * kh + i) * HEAD_DIM:(KV_GROUP * kh + i + 1) * HEAD_DIM]
             for i in range(KV_GROUP)], axis=0)
        q2s = (q2 * (scale * LOG2E)).astype(BF16)

        g2 = _dot_nt(kmean[:, kcols].astype(BF16), q2s)
        g_t = g2[:nblk] + g2[nblk:]
        rank = jnp.zeros_like(g_t)
        for m in range(nblk - 1):
            gm = g_t[m:m + 1, :]
            beats = (gm > g_t) | ((gm == g_t) & (nidx > m))
            inc = jnp.where(m < own, 1.0, 0.0)
            rank = rank + jnp.where(beats, inc, 0.0)
        sel_t = jnp.where(nidx == own, 1.0,
                          jnp.where((nidx < own) & (rank < float(MOBA_TOP_K)), 1.0, 0.0))

        def attend(nb, kh=kh, kcols=kcols, q2s=q2s, sel_t=sel_t):
            m = jnp.full((1, cols), NEG_INF, F32)
            for n in range(nb):
                rows = slice(n * blk, (n + 1) * blk)
                s = _dot_nt(kbf[rows, kcols], q2s)
                s = jnp.where(sel_t[n:n + 1, :] > 0.5, s, NEG_INF)
                if n >= nb - SPAN_BLOCKS:
                    s = jnp.where(kq_diff <= (own - n) * blk, s, NEG_INF)
                s_scr[rows, :] = s
                m = jnp.maximum(m, jnp.max(s, axis=0, keepdims=True))
            for n in range(nb):
                rows = slice(n * blk, (n + 1) * blk)
                p_scr[rows, :] = jnp.exp2(s_scr[rows, :] - m).astype(BF16)
            acc = _dot(vtb[kh, :, :nb * blk], p_scr[:nb * blk, :])
            o = (acc[:HEAD_DIM] / acc[HEAD_DIM:HEAD_DIM + 1]).T.astype(BF16)
            for i in range(KV_GROUP):
                hd = KV_GROUP * kh + i
                o_scr[:, hd * HEAD_DIM:(hd + 1) * HEAD_DIM] = o[i * blk:(i + 1) * blk]

        for nb in range(SPAN_BLOCKS, nblk + SPAN_BLOCKS, SPAN_BLOCKS):
            pl.when((own >= nb - SPAN_BLOCKS) & (own < nb))(functools.partial(attend, min(nb, nblk)))

    out_ref[...] = h_ref[...] + _dot(o_scr[...], wo_ref[...])


def _attn_prompt(q, k, v, h, wo, layer, *, batch, seq):
    blk = MOBA_BLOCK
    nqb = seq // blk
    d = h.shape[1]
    nq = q.shape[1]
    nk = N_KV_HEADS * HEAD_DIM
    cols = KV_GROUP * blk
    return pl.pallas_call(
        _attn_prompt_kernel,
        grid=(batch, nqb),
        in_specs=[
            pl.BlockSpec((blk, nq), lambda b, i: (b * nqb + i, 0)),
            pl.BlockSpec((seq * N_KV_HEADS, HEAD_DIM), lambda b, i: (b, 0)),
            pl.BlockSpec((seq * N_KV_HEADS, HEAD_DIM), lambda b, i: (b, 0)),
            pl.BlockSpec((blk, d), lambda b, i: (b * nqb + i, 0)),
            _layer_spec((nq, d), layer),
        ],
        out_specs=pl.BlockSpec((blk, d), lambda b, i: (b * nqb + i, 0)),
        out_shape=jax.ShapeDtypeStruct(h.shape, F32),
        scratch_shapes=[
            pltpu.VMEM((seq, nk), BF16),
            pltpu.VMEM((N_KV_HEADS, HEAD_DIM + 2 * SUBLANES, seq), BF16),
            pltpu.VMEM((2 * nqb, nk), F32),
            pltpu.VMEM((seq, cols), F32),
            pltpu.VMEM((seq, cols), BF16),
            pltpu.VMEM((blk, nq), BF16),
        ],
        compiler_params=_cparams(("parallel", "arbitrary")),
        name="attn_prompt",
    )(q, k, v, h, wo)


def _attn_sample_kernel(pt_ref, qd_ref, knew_ref, vnew_ref, ck_hbm, cv_hbm, out_ref,
                        kbuf, vbuf, s_scr, p_scr, kmean_scr, ksem, vsem,
                        *, base, npg, dec_seq):
    b = pl.program_id(0)
    nbatch = pl.num_programs(0)
    n_pages, prow, _ = kbuf.shape
    nchunk = n_pages // npg
    ccols = npg * prow
    bcols = MOBA_BLOCK * N_KV_HEADS
    pages_per_blk = bcols // prow
    blk_per_chunk = npg // pages_per_blk
    nrow = qd_ref.shape[0]
    nl = kmean_scr.shape[0]
    nblk = nl // SUBLANES
    scale = HEAD_DIM ** -0.5

    def page_copy(hbm, buf, sem, bb, pg):
        return pltpu.make_async_copy(hbm.at[base + pt_ref[bb, pg]], buf.at[pg], sem.at[pg // npg])

    def fetch(hbm, buf, sem, bb):
        for pg in range(n_pages):
            page_copy(hbm, buf, sem, bb, pg).start()

    def wait_chunk(hbm, buf, sem, c):
        for pg in range(c * npg, (c + 1) * npg):
            page_copy(hbm, buf, sem, b, pg).wait()

    @pl.when(b == 0)
    def _():
        fetch(ck_hbm, kbuf, ksem, 0)
        fetch(cv_hbm, vbuf, vsem, 0)

    qd = qd_ref[...]
    q_hi, q_lo = _split_bf16(qd)
    q_hl = jnp.concatenate([q_hi, q_lo], axis=0)

    for c in range(nchunk):
        wait_chunk(ck_hbm, kbuf, ksem, c)
        kf = [kbuf[c * npg + i] for i in range(npg)]
        for i in range(blk_per_chunk):
            tot = kf[pages_per_blk * i]
            for j in range(1, pages_per_blk):
                tot = tot + kf[pages_per_blk * i + j]
            part = jnp.sum(tot.reshape(prow // SUBLANES, SUBLANES, HEAD_DIM), axis=0)
            part = (part + pltpu.roll(part, N_KV_HEADS, 0)) * (1.0 / MOBA_BLOCK)
            row0 = (c * blk_per_chunk + i) * SUBLANES
            kmean_scr[row0:row0 + SUBLANES, :] = part
        k_hi, k_lo = _split_bf16(jnp.concatenate(kf, axis=0))
        s2 = _dot_nt(q_hl, k_hi)
        s_scr[:, c * ccols:(c + 1) * ccols] = (s2[:nrow] + s2[nrow:]) + _dot_nt(q_hi, k_lo)

    @pl.when(b + 1 < nbatch)
    def _():
        fetch(ck_hbm, kbuf, ksem, b + 1)

    g = _dot3(qd, kmean_scr[...], nt=True)
    lane = lax.broadcasted_iota(jnp.int32, (nrow, nl), 1)
    kv_r = (lax.broadcasted_iota(jnp.int32, (nrow, nl), 0) // dec_seq) // KV_GROUP
    rowsel = [jnp.where(kv_r == kh, 1.0, 0.0) for kh in range(N_KV_HEADS)]
    rank = jnp.zeros_like(g)
    for m in range(nblk):
        for kh in range(N_KV_HEADS):
            pos = m * SUBLANES + kh
            col = g[:, pos:pos + 1]
            beats = (col > g) | ((col == g) & (lane > pos))
            rank = rank + jnp.where(beats, rowsel[kh], 0.0)
    sel = jnp.where((lane % SUBLANES) == kv_r,
                    jnp.where(rank < float(MOBA_TOP_K), 1.0, 0.0), 0.0)
    blkid = lane // SUBLANES

    kv_c = (lax.broadcasted_iota(jnp.int32, (nrow, 1), 0) // dec_seq) // KV_GROUP
    t_row = lax.broadcasted_iota(jnp.int32, (nrow, 1), 0) % dec_seq

    def rows_for(x, t):
        out = jnp.zeros((nrow, HEAD_DIM), F32)
        for kh in range(N_KV_HEADS):
            r = t * N_KV_HEADS + kh
            out = out + jnp.where(kv_c == kh, x[r:r + 1, :], 0.0)
        return out

    knew = knew_ref[...]
    vnew = vnew_ref[...]
    s_own = [jnp.sum(qd * rows_for(knew, t), axis=-1, keepdims=True) * scale
             for t in range(dec_seq)]
    ok_own = [t_row >= t for t in range(dec_seq)]

    kh_match = (lax.broadcasted_iota(jnp.int32, (nrow, bcols), 1) % N_KV_HEADS) == (
        (lax.broadcasted_iota(jnp.int32, (nrow, bcols), 0) // dec_seq) // KV_GROUP)
    m = jnp.full((nrow, 1), NEG_INF, F32)
    for t in range(dec_seq):
        m = jnp.maximum(m, jnp.where(ok_own[t], s_own[t], NEG_INF))
    for n in range(nblk):
        cs = slice(n * bcols, (n + 1) * bcols)
        pick = jnp.max(jnp.where(blkid == n, sel, 0.0), axis=1, keepdims=True)
        sc = jnp.where(pick > 0.5, s_scr[:, cs] * scale, NEG_INF)
        sc = jnp.where(kh_match, sc, NEG_INF)
        s_scr[:, cs] = sc
        m = jnp.maximum(m, jnp.max(sc, axis=-1, keepdims=True))
    l = jnp.zeros((nrow, 1), F32)
    for n in range(nblk):
        cs = slice(n * bcols, (n + 1) * bcols)
        p = jnp.exp(s_scr[:, cs] - m)
        l = l + jnp.sum(p, axis=-1, keepdims=True)
        p_hi, p_lo = _split_bf16(p)
        p_scr[:nrow, cs] = p_hi
        p_scr[nrow:, cs] = p_lo
    acc = jnp.zeros((nrow, HEAD_DIM), F32)
    for t in range(dec_seq):
        p_t = jnp.where(ok_own[t], jnp.exp(s_own[t] - m), 0.0)
        l = l + p_t
        acc = acc + p_t * rows_for(vnew, t)

    for c in range(nchunk):
        wait_chunk(cv_hbm, vbuf, vsem, c)
        vb = jnp.concatenate([vbuf[c * npg + i].astype(BF16) for i in range(npg)], axis=0)
        pv = _dot(p_scr[:, c * ccols:(c + 1) * ccols], vb)
        acc = acc + (pv[:nrow] + pv[nrow:])

    @pl.when(b + 1 < nbatch)
    def _():
        fetch(cv_hbm, vbuf, vsem, b + 1)

    out_ref[...] = acc / l


def _attn_sample(q, k_new, v_new, cache_k, cache_v, page_table, layer, *, n_pool, batch, dec_seq,
                 npg=8):
    n_pages = page_table.shape[1]
    prow = cache_k.shape[1]
    page = prow // N_KV_HEADS
    assert (n_pages * page) % MOBA_BLOCK == 0 and dec_seq <= MOBA_BLOCK
    assert n_pages % npg == 0 and MOBA_BLOCK % page == 0 and npg % (MOBA_BLOCK // page) == 0
    assert SUBLANES == 2 * N_KV_HEADS and prow % SUBLANES == 0
    nchunk = n_pages // npg
    nblk = n_pages * page // MOBA_BLOCK
    nrow = N_HEADS * dec_seq
    qd = q.reshape(batch, dec_seq, N_HEADS, HEAD_DIM).transpose(0, 2, 1, 3)
    qd = qd.reshape(batch, nrow, HEAD_DIM)
    per_b = lambda b, pt: (b, 0, 0)
    new_rows = dec_seq * N_KV_HEADS
    grid_spec = pltpu.PrefetchScalarGridSpec(
        num_scalar_prefetch=1,
        grid=(batch,),
        in_specs=[
            pl.BlockSpec((None, nrow, HEAD_DIM), per_b),
            pl.BlockSpec((None, new_rows, HEAD_DIM), per_b),
            pl.BlockSpec((None, new_rows, HEAD_DIM), per_b),
            pl.BlockSpec(memory_space=pl.ANY),
            pl.BlockSpec(memory_space=pl.ANY),
        ],
        out_specs=pl.BlockSpec((None, nrow, HEAD_DIM), per_b),
        scratch_shapes=[
            pltpu.VMEM((n_pages, prow, HEAD_DIM), F32),
            pltpu.VMEM((n_pages, prow, HEAD_DIM), F32),
            pltpu.VMEM((nrow, n_pages * prow), F32),
            pltpu.VMEM((2 * nrow, n_pages * prow), BF16),
            pltpu.VMEM((nblk * SUBLANES, HEAD_DIM), F32),
            pltpu.SemaphoreType.DMA((nchunk,)),
            pltpu.SemaphoreType.DMA((nchunk,)),
        ],
    )
    o = pl.pallas_call(
        functools.partial(_attn_sample_kernel, base=layer * n_pool, npg=npg, dec_seq=dec_seq),
        grid_spec=grid_spec,
        out_shape=jax.ShapeDtypeStruct((batch, nrow, HEAD_DIM), F32),
        compiler_params=_cparams(("arbitrary",)),
        name="attn_sample",
    )(page_table, qd, k_new.reshape(batch, new_rows, HEAD_DIM),
      v_new.reshape(batch, new_rows, HEAD_DIM), cache_k, cache_v)
    o = o.reshape(batch, N_HEADS, dec_seq, HEAD_DIM).transpose(0, 2, 1, 3)
    return o.reshape(batch * dec_seq, N_HEADS * HEAD_DIM)


def _oproj_kernel(o_ref, h_ref, wo_ref, out_ref):
    out_ref[...] = h_ref[...] + _dot3(o_ref[...], wo_ref[...])


def _oproj(o, h, wo, layer):
    n, d = h.shape
    full = pl.BlockSpec((n, d), lambda i: (0, 0))
    return pl.pallas_call(
        _oproj_kernel,
        grid=(1,),
        in_specs=[pl.BlockSpec(o.shape, lambda i: (0, 0)), full,
                  _layer_spec(wo.shape[1:], layer)],
        out_specs=full,
        out_shape=jax.ShapeDtypeStruct(h.shape, F32),
        compiler_params=_cparams(("arbitrary",)),
        name="oproj_sample",
    )(o, h, wo)


def _route(logits, lane):
    first_expert_lane = N_EXPERT_GROUPS
    is_group = lane < N_EXPERT_GROUPS
    gl = jnp.where(is_group, logits, NEG_INF)
    ge = jnp.where(is_group, jnp.exp(gl - jnp.max(gl, axis=-1, keepdims=True)), 0.0)
    gprob = ge / jnp.sum(ge, axis=-1, keepdims=True)
    g_w = jnp.max(gprob, axis=-1, keepdims=True)
    g_idx = jnp.min(jnp.where(is_group & (gprob == g_w), lane, LANES), axis=-1, keepdims=True)
    lo = first_expert_lane + g_idx * EXPERTS_PER_GROUP
    in_group = (lane >= lo) & (lane < lo + EXPERTS_PER_GROUP)
    el = jnp.where(in_group, logits, NEG_INF)
    ee = jnp.where(in_group, jnp.exp(el - jnp.max(el, axis=-1, keepdims=True)), 0.0)
    eprob = jnp.where(in_group, ee / jnp.sum(ee, axis=-1, keepdims=True), -1.0)
    p1 = jnp.max(eprob, axis=-1, keepdims=True)
    i1 = jnp.min(jnp.where(eprob == p1, lane, LANES), axis=-1, keepdims=True)
    rest = jnp.where(lane == i1, -1.0, eprob)
    p2 = jnp.max(rest, axis=-1, keepdims=True)
    i2 = jnp.min(jnp.where(rest == p2, lane, LANES), axis=-1, keepdims=True)
    denom = p1 + p2
    w1 = p1 / denom * g_w
    w2 = p2 / denom * g_w
    return g_idx, jnp.where(lane == i1, w1, 0.0) + jnp.where(lane == i2, w2, 0.0)


def _moe_kernel(h_ref, n2_ref, wr_ref, br_ref, wg_ref, wu_ref, wd_ref, out_ref,
                xn_scr, gates_scr, acc_scr, hid_scr, *, precise):
    g = pl.program_id(1)
    ngroups = pl.num_programs(1)
    tm = h_ref.shape[0]
    de = wg_ref.shape[2]
    lane = lax.broadcasted_iota(jnp.int32, (tm, LANES), 1)
    first_expert_lane = N_EXPERT_GROUPS

    @pl.when(g == 0)
    def _():
        x = h_ref[...]
        xn = _rms(x, n2_ref[...])
        xn_scr[...] = xn.astype(xn_scr.dtype)
        acc_scr[...] = x
        _, gates_scr[...] = _route(_dot3(xn, wr_ref[...]) + br_ref[...], lane)

    xb = xn_scr[...]
    gates = gates_scr[...]
    for e in range(EXPERTS_PER_GROUP):
        a = _mm(xb, wg_ref[e], precise)
        b = _mm(xb, wu_ref[e], precise)
        tgt = first_expert_lane + g * EXPERTS_PER_GROUP + e
        gate = jnp.sum(jnp.where(lane == tgt, gates, 0.0), axis=-1, keepdims=True)
        hid_scr[:, e * de:(e + 1) * de] = (_silu(a) * b * gate).astype(hid_scr.dtype)
    acc_scr[...] += _mm(hid_scr[...], wd_ref[...], precise)

    @pl.when(g == ngroups - 1)
    def _():
        out_ref[...] = acc_scr[...]


def _moe(h, n2, wr, br, wg, wu, wd, layer, *, tm, precise):
    n, d = h.shape
    de = wg.shape[3]
    epg = EXPERTS_PER_GROUP
    const = lambda i, g: (0, 0)
    group = lambda i, g: (layer, g, 0, 0)
    wd = wd.reshape(wd.shape[0], N_EXPERT_GROUPS, epg * de, d)
    work = F32 if precise else BF16
    return pl.pallas_call(
        functools.partial(_moe_kernel, precise=precise),
        grid=(n // tm, N_EXPERT_GROUPS),
        in_specs=[
            pl.BlockSpec((tm, d), lambda i, g: (i, 0)),
            pl.BlockSpec((1, d), const),
            pl.BlockSpec((d, LANES), const),
            pl.BlockSpec((1, LANES), const),
            pl.BlockSpec((None, epg, d, de), group),
            pl.BlockSpec((None, epg, d, de), group),
            pl.BlockSpec((None, None, epg * de, d), group),
        ],
        out_specs=pl.BlockSpec((tm, d), lambda i, g: (i, 0)),
        out_shape=jax.ShapeDtypeStruct((n, d), F32),
        scratch_shapes=[
            pltpu.VMEM((tm, d), work),
            pltpu.VMEM((tm, LANES), F32),
            pltpu.VMEM((tm, d), F32),
            pltpu.VMEM((tm, epg * de), work),
        ],
        compiler_params=_cparams(("parallel", "arbitrary")),
        name="moe_sample" if precise else "moe",
    )(h, n2, wr, br, wg, wu, wd)


def kernel(x_prompt, x_sample, cache_k, cache_v, page_table, norm1, norm2, a_w_in, a_g_v, a_w_s, a_b_s, a_w_out, b_w_qkv, b_q_norm, b_k_norm, b_w_o, moe_w_group, moe_b_group, moe_w_router, moe_b_router, moe_w_gate, moe_w_up, moe_w_down):
    bp, tp, d = x_prompt.shape
    bs, ts, _ = x_sample.shape
    depth = norm1.shape[0]
    n_attn, n_pool, page = cache_k.shape[0], cache_k.shape[1], cache_k.shape[2]
    past_len = page_table.shape[1] * page
    assert CHUNK % ts == 0 and (bs * ts) % CHUNK == 0 and tp % MOBA_BLOCK == 0

    hp = x_prompt.reshape(bp * tp, d)
    hs = x_sample.reshape(bs * ts, d)
    ck = cache_k.reshape(n_attn * n_pool, page * N_KV_HEADS, HEAD_DIM)
    cv = cache_v.reshape(n_attn * n_pool, page * N_KV_HEADS, HEAD_DIM)
    tab_p = _rope_tables(jnp.arange(tp, dtype=jnp.int32))
    pos_s = past_len + jnp.arange(ts, dtype=jnp.int32)
    tab_s = tuple(jnp.tile(t, (bs, 1)) for t in _rope_tables(pos_s))

    pad = LANES - N_EXPERT_GROUPS - N_EXPERTS
    wr_all = jnp.concatenate(
        [moe_w_group, moe_w_router, jnp.zeros((depth, d, pad), F32)], axis=2)
    br_all = jnp.concatenate(
        [moe_b_group, moe_b_router, jnp.zeros((depth, pad), F32)], axis=1)[:, None, :]

    w_in_bf, w_out_bf = a_w_in.astype(BF16), a_w_out.astype(BF16)
    w_qkv_bf, w_o_bf = b_w_qkv.astype(BF16), b_w_o.astype(BF16)
    wg_bf, wu_bf, wd_bf = moe_w_gate.astype(BF16), moe_w_up.astype(BF16), moe_w_down.astype(BF16)

    new_k_p, new_v_p, new_k_s, new_v_s, new_chunk_v = [], [], [], [], []
    reps = CHUNK // ts
    for i in range(depth):
        j = i // 2
        n1 = norm1[i][None, :]
        if i % 2 == 0:
            g_v = a_g_v[j][None, :]
            hp = _mixer_a(hp, n1, w_in_bf, g_v, a_w_s[j], a_b_s[j][:, :, None], w_out_bf, j,
                          tm=512, seq_len=CHUNK, emit_v=False, precise=False)
            wmix_s = jnp.tile(a_w_s[j][:, :ts, :ts], (1, reps, reps))
            bs_s = jnp.tile(a_b_s[j][:, :ts], (1, reps))[:, :, None]
            hs, v_rows = _mixer_a(hs, n1, a_w_in, g_v, wmix_s, bs_s, a_w_out, j,
                                  tm=CHUNK, seq_len=ts, emit_v=True, precise=True)
            new_chunk_v.append(v_rows.reshape(bs, ts, -1))
        else:
            qn = b_q_norm[j][None, :]
            kn = b_k_norm[j][None, :]
            qp, kp, vp = _qkv(hp, n1, w_qkv_bf, j, qn, kn, tab_p, tm=512, precise=False)
            qs, ks_new, vs_new = _qkv(hs, n1, b_w_qkv, j, qn, kn, tab_s, tm=bs * ts, precise=True)
            hp = _attn_prompt(qp, kp, vp, hp, w_o_bf, j, batch=bp, seq=tp)
            o_s = _attn_sample(qs, ks_new, vs_new, ck, cv, page_table, j, n_pool=n_pool,
                               batch=bs, dec_seq=ts)
            hs = _oproj(o_s, hs, b_w_o, j)
            new_k_p.append(kp.reshape(bp, tp, N_KV_HEADS, HEAD_DIM))
            new_v_p.append(vp.reshape(bp, tp, N_KV_HEADS, HEAD_DIM))
            new_k_s.append(ks_new.reshape(bs, ts, N_KV_HEADS, HEAD_DIM))
            new_v_s.append(vs_new.reshape(bs, ts, N_KV_HEADS, HEAD_DIM))
        n2 = norm2[i][None, :]
        hp = _moe(hp, n2, wr_all[i], br_all[i], wg_bf, wu_bf, wd_bf, i,
                  tm=min(1024, bp * tp), precise=False)
        hs = _moe(hs, n2, wr_all[i], br_all[i], moe_w_gate, moe_w_up, moe_w_down, i,
                  tm=bs * ts, precise=True)
    return (hp.reshape(bp, tp, d), hs.reshape(bs, ts, d),
            jnp.stack(new_k_p), jnp.stack(new_v_p), jnp.stack(new_k_s), jnp.stack(new_v_s),
            jnp.stack(new_chunk_v))
```

```python
import functools

import jax
import jax.numpy as jnp
from jax import lax
from jax.experimental import pallas as pl
from jax.experimental.pallas import tpu as pltpu

F32 = jnp.float32
BF16 = jnp.bfloat16

CHUNK = 128
CHUNK_GROUPS = 4
HEAD_DIM = 128
N_HEADS = 8
N_KV_HEADS = 4
KV_GROUP = N_HEADS // N_KV_HEADS
ROT_DIM = HEAD_DIM // 4
ROPE_THETA = 500000.0
MOBA_BLOCK = 256
MOBA_TOP_K = 3
N_EXPERT_GROUPS = 4
EXPERTS_PER_GROUP = 4
N_EXPERTS = N_EXPERT_GROUPS * EXPERTS_PER_GROUP
NORM_EPS = 1e-6
NEG_INF = -1e30
LOG2E = 1.4426950408889634
SPAN_BLOCKS = 1

SUBLANES = 8
LANES = 128
VMEM_LIMIT_BYTES = 56 * 1024 * 1024

_NT = (((1,), (1,)), ((), ()))


def _dot(a, b):
    return jnp.dot(a, b, preferred_element_type=F32)


def _dot_nt(a, b):
    return lax.dot_general(a, b, _NT, preferred_element_type=F32)


def _split_bf16(a):
    hi = a.astype(BF16)
    lo = (a - hi.astype(F32)).astype(BF16)
    return hi, lo


def _dot3(a, b, nt=False):
    d = _dot_nt if nt else _dot
    ah, al = _split_bf16(a)
    bh, bl = _split_bf16(b)
    return d(ah, bh) + (d(ah, bl) + d(al, bh))


def _mm(x, w, precise):
    if precise:
        return _dot3(x, w)
    return _dot(x.astype(BF16), w.astype(BF16))


def _rms(x, g):
    ms = jnp.mean(x * x, axis=-1, keepdims=True)
    return x * lax.rsqrt(ms + NORM_EPS) * g


def _gelu_tanh(x):
    c = 0.7978845608028654
    return 0.5 * x * (1.0 + jnp.tanh(c * (x + 0.044715 * (x * x * x))))


def _silu(x):
    return x * (1.0 / (1.0 + jnp.exp(-x)))


def _cparams(sem):
    return pltpu.CompilerParams(dimension_semantics=sem, vmem_limit_bytes=VMEM_LIMIT_BYTES)


def _layer_spec(shape, layer):
    zeros = (0,) * len(shape)
    return pl.BlockSpec((None,) + tuple(shape), lambda *_: (layer,) + zeros)


def _mixer_a_kernel(h_ref, n1_ref, win_ref, gv_ref, wmix_ref, bs_ref, wout_ref, *rest,
                    seq_len, emit_v, precise, nsplit):
    if emit_v:
        out_ref, v_ref, us_scr = rest
    else:
        out_ref, us_scr = rest
    width = gv_ref.shape[1]
    gdim = width // CHUNK_GROUPS
    sub = h_ref.shape[0] // nsplit
    row = lax.broadcasted_iota(jnp.int32, (CHUNK, CHUNK), 0)
    col = lax.broadcasted_iota(jnp.int32, (CHUNK, CHUNK), 1)
    mask = col <= row
    if seq_len < CHUNK:
        mask = mask & ((row // seq_len) == (col // seq_len))
    wms = [jnp.where(mask, wmix_ref[g], 0.0) for g in range(CHUNK_GROUPS)]
    for sl in range(nsplit):
        srows = slice(sl * sub, (sl + 1) * sub)
        x = h_ref[srows, :]
        xn = _rms(x, n1_ref[...])
        z = _gelu_tanh(_mm(xn, win_ref[...], precise))
        u = z[:, :width]
        vn = _rms(z[:, width:], gv_ref[...])
        if emit_v:
            v_ref[srows, :] = vn
        if not precise:
            vn = vn.astype(BF16)
        for g in range(CHUNK_GROUPS):
            bias = bs_ref[g]
            for c in range(sub // CHUNK):
                rows = slice(c * CHUNK, (c + 1) * CHUNK)
                cols = slice(g * gdim, (g + 1) * gdim)
                s = _mm(wms[g], vn[rows, cols], precise) + bias
                us_scr[sl * sub + c * CHUNK:sl * sub + (c + 1) * CHUNK, cols] = (
                    (u[rows, cols] * s).astype(us_scr.dtype))
        out_ref[srows, :] = x + _mm(us_scr[srows, :], wout_ref[...], precise)


def _mixer_a(h, n1, w_in, g_v, wmix, bs, w_out, layer, *, tm, seq_len, emit_v, precise, nsplit=1):
    n, d = h.shape
    width = g_v.shape[1]
    const = lambda i: (0, 0)
    out_shape = [jax.ShapeDtypeStruct((n, d), F32)]
    out_specs = [pl.BlockSpec((tm, d), lambda i: (i, 0))]
    if emit_v:
        out_shape.append(jax.ShapeDtypeStruct((n, width), F32))
        out_specs.append(pl.BlockSpec((tm, width), lambda i: (i, 0)))
    res = pl.pallas_call(
        functools.partial(_mixer_a_kernel, seq_len=seq_len, emit_v=emit_v, precise=precise,
                          nsplit=nsplit),
        grid=(n // tm,),
        in_specs=[
            pl.BlockSpec((tm, d), lambda i: (i, 0)),
            pl.BlockSpec((1, d), const),
            _layer_spec((d, 2 * width), layer),
            pl.BlockSpec((1, width), const),
            pl.BlockSpec((CHUNK_GROUPS, CHUNK, CHUNK), lambda i: (0, 0, 0)),
            pl.BlockSpec((CHUNK_GROUPS, CHUNK, 1), lambda i: (0, 0, 0)),
            _layer_spec((width, d), layer),
        ],
        out_specs=out_specs,
        out_shape=out_shape,
        scratch_shapes=[pltpu.VMEM((tm, width), F32 if precise else BF16)],
        compiler_params=_cparams(("parallel",)),
        name="mixer_a_sample" if emit_v else "mixer_a",
    )(h, n1, w_in, g_v, wmix, bs, w_out)
    return res if emit_v else res[0]


def _qkv_kernel(h_ref, n1_ref, w_ref, qn_ref, kn_ref, cos_ref, sa_ref, sb_ref,
                q_ref, k_ref, v_ref, *, precise, nsplit):
    nq = q_ref.shape[1]
    nk = N_KV_HEADS * HEAD_DIM
    sub = h_ref.shape[0] // nsplit
    for s in range(nsplit):
        rows = slice(s * sub, (s + 1) * sub)
        xn = _rms(h_ref[rows, :], n1_ref[...])
        qkv = _mm(xn, w_ref[...], precise)
        cos = cos_ref[rows, :]
        sa = sa_ref[rows, :]
        sb = sb_ref[rows, :]

        def head(t, g, cos=cos, sa=sa, sb=sb):
            t = _rms(t, g)
            return (t * cos + pltpu.roll(t, HEAD_DIM - ROT_DIM // 2, 1) * sa
                    + pltpu.roll(t, ROT_DIM // 2, 1) * sb)

        for hd in range(nq // HEAD_DIM):
            cols = slice(hd * HEAD_DIM, (hd + 1) * HEAD_DIM)
            q_ref[rows, cols] = head(qkv[:, cols], qn_ref[...])
        for hd in range(N_KV_HEADS):
            kv_rows = pl.ds(s * sub * N_KV_HEADS + hd, sub, stride=N_KV_HEADS)
            k_ref[kv_rows, :] = head(qkv[:, nq + hd * HEAD_DIM: nq + (hd + 1) * HEAD_DIM],
                                     kn_ref[...])
            v_ref[kv_rows, :] = qkv[:, nq + nk + hd * HEAD_DIM: nq + nk + (hd + 1) * HEAD_DIM]


def _rope_tables(pos):
    half = ROT_DIM // 2
    inv_freq = jnp.power(ROPE_THETA, -jnp.arange(half, dtype=F32) * 2.0 / ROT_DIM)
    ang = pos.astype(F32)[:, None] * inv_freq[None, :]
    cos = jnp.cos(ang)
    sin = jnp.sin(ang)
    t = pos.shape[0]
    ones = jnp.ones((t, HEAD_DIM - ROT_DIM), F32)
    zeros = jnp.zeros((t, HEAD_DIM - ROT_DIM), F32)
    z16 = jnp.zeros((t, half), F32)
    cos_t = jnp.concatenate([cos, cos, ones], axis=1)
    sa_t = jnp.concatenate([-sin, z16, zeros], axis=1)
    sb_t = jnp.concatenate([z16, sin, zeros], axis=1)
    return cos_t, sa_t, sb_t


def _qkv(h, n1, w, layer, qn, kn, tables, *, tm, precise, nsplit=1):
    n, d = h.shape
    nq = N_HEADS * HEAD_DIM
    nk = N_KV_HEADS * HEAD_DIM
    t = tables[0].shape[0]
    tiles_per_seq = t // tm
    const = lambda i: (0, 0)
    tab = pl.BlockSpec((tm, HEAD_DIM), lambda i: (i % tiles_per_seq, 0))
    return pl.pallas_call(
        functools.partial(_qkv_kernel, precise=precise, nsplit=nsplit),
        grid=(n // tm,),
        in_specs=[
            pl.BlockSpec((tm, d), lambda i: (i, 0)),
            pl.BlockSpec((1, d), const),
            _layer_spec((d, nq + 2 * nk), layer),
            pl.BlockSpec((1, HEAD_DIM), const),
            pl.BlockSpec((1, HEAD_DIM), const),
            tab, tab, tab,
        ],
        out_specs=[
            pl.BlockSpec((tm, nq), lambda i: (i, 0)),
            pl.BlockSpec((tm * N_KV_HEADS, HEAD_DIM), lambda i: (i, 0)),
            pl.BlockSpec((tm * N_KV_HEADS, HEAD_DIM), lambda i: (i, 0)),
        ],
        out_shape=[
            jax.ShapeDtypeStruct((n, nq), F32),
            jax.ShapeDtypeStruct((n * N_KV_HEADS, HEAD_DIM), F32),
            jax.ShapeDtypeStruct((n * N_KV_HEADS, HEAD_DIM), F32),
        ],
        compiler_params=_cparams(("parallel",)),
        name="qkv_sample" if precise else "qkv",
    )(h, n1, w, qn, kn, *tables)


def _attn_prompt_kernel(q_ref, k_ref, v_ref, h_ref, wo_ref, out_ref,
                        kbf, vtb, kmean, s_scr, p_scr, o_scr):
    blk = MOBA_BLOCK
    nblk = kbf.shape[0] // blk
    own = pl.program_id(1)
    scale = HEAD_DIM ** -0.5

    @pl.when(own == 0)
    def _():
        for n in range(nblk):
            rows = slice(n * blk, (n + 1) * blk)
            for kh in range(N_KV_HEADS):
                kcols = slice(kh * HEAD_DIM, (kh + 1) * HEAD_DIM)
                src_rows = pl.ds(n * blk * N_KV_HEADS + kh, blk, stride=N_KV_HEADS)
                kk = k_ref[src_rows, :]
                kbf[rows, kcols] = kk.astype(BF16)
                km_hi, km_lo = _split_bf16(jnp.mean(kk, axis=0, keepdims=True))
                kmean[n:n + 1, kcols] = km_hi.astype(F32)
                kmean[nblk + n:nblk + n + 1, kcols] = km_lo.astype(F32)
                vtb[kh, :HEAD_DIM, rows] = v_ref[src_rows, :].T.astype(BF16)
        for kh in range(N_KV_HEADS):
            vtb[kh, HEAD_DIM:, :] = jnp.ones((vtb.shape[1] - HEAD_DIM, vtb.shape[2]), BF16)

    cols = KV_GROUP * blk
    nidx = lax.broadcasted_iota(jnp.int32, (nblk, cols), 0)
    kq_diff = (lax.broadcasted_iota(jnp.int32, (blk, cols), 0)
               - lax.broadcasted_iota(jnp.int32, (blk, cols), 1) % blk)

    for kh in range(N_KV_HEADS):
        kcols = slice(kh * HEAD_DIM, (kh + 1) * HEAD_DIM)
        q2 = jnp.concatenate(
            [q_ref[:, (KV_GROUP * kh + i) * HEAD_DIM:(KV_GROUP * kh + i + 1) * HEAD_DIM]
             for i in range(KV_GROUP)], axis=0)
        q2s = (q2 * (scale * LOG2E)).astype(BF16)

        g2 = _dot_nt(kmean[:, kcols].astype(BF16), q2s)
        g_t = g2[:nblk] + g2[nblk:]
        rank = jnp.zeros_like(g_t)
        for m in range(nblk - 1):
            gm = g_t[m:m + 1, :]
            beats = (gm > g_t) | ((gm == g_t) & (nidx > m))
            inc = jnp.where(m < own, 1.0, 0.0)
            rank = rank + jnp.where(beats, inc, 0.0)
        sel_t = jnp.where(nidx == own, 1.0,
                          jnp.where((nidx < own) & (rank < float(MOBA_TOP_K)), 1.0, 0.0))

        def attend(nb, kh=kh, kcols=kcols, q2s=q2s, sel_t=sel_t):
            m = jnp.full((1, cols), NEG_INF, F32)
            for n in range(nb):
                rows = slice(n * blk, (n + 1) * blk)
                s = _dot_nt(kbf[rows, kcols], q2s)
                s = jnp.where(sel_t[n:n + 1, :] > 0.5, s, NEG_INF)
                if n >= nb - SPAN_BLOCKS:
                    s = jnp.where(kq_diff <= (own - n) * blk, s, NEG_INF)
                s_scr[rows, :] = s
                m = jnp.maximum(m, jnp.max(s, axis=0, keepdims=True))
            for n in range(nb):
                rows = slice(n * blk, (n + 1) * blk)
                p_scr[rows, :] = jnp.exp2(s_scr[rows, :] - m).astype(BF16)
            acc = _dot(vtb[kh, :, :nb * blk], p_scr[:nb * blk, :])
            o = (acc[:HEAD_DIM] / acc[HEAD_DIM:HEAD_DIM + 1]).T.astype(BF16)
            for i in range(KV_GROUP):
                hd = KV_GROUP * kh + i
                o_scr[:, hd * HEAD_DIM:(hd + 1) * HEAD_DIM] = o[i * blk:(i + 1) * blk]

        for nb in range(SPAN_BLOCKS, nblk + SPAN_BLOCKS, SPAN_BLOCKS):
            pl.when((own >= nb - SPAN_BLOCKS) & (own < nb))(functools.partial(attend, min(nb, nblk)))

    out_ref[...] = h_ref[...] + _dot(o_scr[...], wo_ref[...])


def _attn_prompt(q, k, v, h, wo, layer, *, batch, seq):
    blk = MOBA_BLOCK
    nqb = seq // blk
    d = h.shape[1]
    nq = q.shape[1]
    nk = N_KV_HEADS * HEAD_DIM
    cols = KV_GROUP * blk
    return pl.pallas_call(
        _attn_prompt_kernel,
        grid=(batch, nqb),
        in_specs=[
            pl.BlockSpec((blk, nq), lambda b, i: (b * nqb + i, 0)),
            pl.BlockSpec((seq * N_KV_HEADS, HEAD_DIM), lambda b, i: (b, 0)),
            pl.BlockSpec((seq * N_KV_HEADS, HEAD_DIM), lambda b, i: (b, 0)),
            pl.BlockSpec((blk, d), lambda b, i: (b * nqb + i, 0)),
            _layer_spec((nq, d), layer),
        ],
        out_specs=pl.BlockSpec((blk, d), lambda b, i: (b * nqb + i, 0)),
        out_shape=jax.ShapeDtypeStruct(h.shape, F32),
        scratch_shapes=[
            pltpu.VMEM((seq, nk), BF16),
            pltpu.VMEM((N_KV_HEADS, HEAD_DIM + 2 * SUBLANES, seq), BF16),
            pltpu.VMEM((2 * nqb, nk), F32),
            pltpu.VMEM((seq, cols), F32),
            pltpu.VMEM((seq, cols), BF16),
            pltpu.VMEM((blk, nq), BF16),
        ],
        compiler_params=_cparams(("parallel", "arbitrary")),
        name="attn_prompt",
    )(q, k, v, h, wo)


def _attn_sample_kernel(pt_ref, qd_ref, knew_ref, vnew_ref, ck_hbm, cv_hbm, out_ref,
                        kbuf, vbuf, s_scr, p_scr, kmean_scr, ksem, vsem,
                        *, base, npg, dec_seq):
    b = pl.program_id(0)
    nbatch = pl.num_programs(0)
    n_pages, prow, _ = kbuf.shape
    nchunk = n_pages // npg
    ccols = npg * prow
    bcols = MOBA_BLOCK * N_KV_HEADS
    pages_per_blk = bcols // prow
    blk_per_chunk = npg // pages_per_blk
    nrow = qd_ref.shape[0]
    nl = kmean_scr.shape[0]
    nblk = nl // SUBLANES
    scale = HEAD_DIM ** -0.5

    def page_copy(hbm, buf, sem, bb, pg):
        return pltpu.make_async_copy(hbm.at[base + pt_ref[bb, pg]], buf.at[pg], sem.at[pg // npg])

    def fetch(hbm, buf, sem, bb):
        for pg in range(n_pages):
            page_copy(hbm, buf, sem, bb, pg).start()

    def wait_chunk(hbm, buf, sem, c):
        for pg in range(c * npg, (c + 1) * npg):
            page_copy(hbm, buf, sem, b, pg).wait()

    @pl.when(b == 0)
    def _():
        fetch(ck_hbm, kbuf, ksem, 0)
        fetch(cv_hbm, vbuf, vsem, 0)

    qd = qd_ref[...]
    q_hi, q_lo = _split_bf16(qd)
    q_hl = jnp.concatenate([q_hi, q_lo], axis=0)

    for c in range(nchunk):
        wait_chunk(ck_hbm, kbuf, ksem, c)
        kf = [kbuf[c * npg + i] for i in range(npg)]
        for i in range(blk_per_chunk):
            tot = kf[pages_per_blk * i]
            for j in range(1, pages_per_blk):
                tot = tot + kf[pages_per_blk * i + j]
            part = jnp.sum(tot.reshape(prow // SUBLANES, SUBLANES, HEAD_DIM), axis=0)
            part = (part + pltpu.roll(part, N_KV_HEADS, 0)) * (1.0 / MOBA_BLOCK)
            row0 = (c * blk_per_chunk + i) * SUBLANES
            kmean_scr[row0:row0 + SUBLANES, :] = part
        k_hi, k_lo = _split_bf16(jnp.concatenate(kf, axis=0))
        s2 = _dot_nt(q_hl, k_hi)
        s_scr[:, c * ccols:(c + 1) * ccols] = (s2[:nrow] + s2[nrow:]) + _dot_nt(q_hi, k_lo)

    @pl.when(b + 1 < nbatch)
    def _():
        fetch(ck_hbm, kbuf, ksem, b + 1)

    g = _dot3(qd, kmean_scr[...], nt=True)
    lane = lax.broadcasted_iota(jnp.int32, (nrow, nl), 1)
    kv_r = (lax.broadcasted_iota(jnp.int32, (nrow, nl), 0) // dec_seq) // KV_GROUP
    rowsel = [jnp.where(kv_r == kh, 1.0, 0.0) for kh in range(N_KV_HEADS)]
    rank = jnp.zeros_like(g)
    for m in range(nblk):
        for kh in range(N_KV_HEADS):
            pos = m * SUBLANES + kh
            col = g[:, pos:pos + 1]
            beats = (col > g) | ((col == g) & (lane > pos))
            rank = rank + jnp.where(beats, rowsel[kh], 0.0)
    sel = jnp.where((lane % SUBLANES) == kv_r,
                    jnp.where(rank < float(MOBA_TOP_K), 1.0, 0.0), 0.0)
    blkid = lane // SUBLANES

    kv_c = (lax.broadcasted_iota(jnp.int32, (nrow, 1), 0) // dec_seq) // KV_GROUP
    t_row = lax.broadcasted_iota(jnp.int32, (nrow, 1), 0) % dec_seq

    def rows_for(x, t):
        out = jnp.zeros((nrow, HEAD_DIM), F32)
        for kh in range(N_KV_HEADS):
            r = t * N_KV_HEADS + kh
            out = out + jnp.where(kv_c == kh, x[r:r + 1, :], 0.0)
        return out

    knew = knew_ref[...]
    vnew = vnew_ref[...]
    s_own = [jnp.sum(qd * rows_for(knew, t), axis=-1, keepdims=True) * scale
             for t in range(dec_seq)]
    ok_own = [t_row >= t for t in range(dec_seq)]

    kh_match = (lax.broadcasted_iota(jnp.int32, (nrow, bcols), 1) % N_KV_HEADS) == (
        (lax.broadcasted_iota(jnp.int32, (nrow, bcols), 0) // dec_seq) // KV_GROUP)
    m = jnp.full((nrow, 1), NEG_INF, F32)
    for t in range(dec_seq):
        m = jnp.maximum(m, jnp.where(ok_own[t], s_own[t], NEG_INF))
    for n in range(nblk):
        cs = slice(n * bcols, (n + 1) * bcols)
        pick = jnp.max(jnp.where(blkid == n, sel, 0.0), axis=1, keepdims=True)
        sc = jnp.where(pick > 0.5, s_scr[:, cs] * scale, NEG_INF)
        sc = jnp.where(kh_match, sc, NEG_INF)
        s_scr[:, cs] = sc
        m = jnp.maximum(m, jnp.max(sc, axis=-1, keepdims=True))
    l = jnp.zeros((nrow, 1), F32)
    for n in range(nblk):
        cs = slice(n * bcols, (n + 1) * bcols)
        p = jnp.exp(s_scr[:, cs] - m)
        l = l + jnp.sum(p, axis=-1, keepdims=True)
        p_hi, p_lo = _split_bf16(p)
        p_scr[:nrow, cs] = p_hi
        p_scr[nrow:, cs] = p_lo
    acc = jnp.zeros((nrow, HEAD_DIM), F32)
    for t in range(dec_seq):
        p_t = jnp.where(ok_own[t], jnp.exp(s_own[t] - m), 0.0)
        l = l + p_t
        acc = acc + p_t * rows_for(vnew, t)

    for c in range(nchunk):
        wait_chunk(cv_hbm, vbuf, vsem, c)
        vb = jnp.concatenate([vbuf[c * npg + i].astype(BF16) for i in range(npg)], axis=0)
        pv = _dot(p_scr[:, c * ccols:(c + 1) * ccols], vb)
        acc = acc + (pv[:nrow] + pv[nrow:])

    @pl.when(b + 1 < nbatch)
    def _():
        fetch(cv_hbm, vbuf, vsem, b + 1)

    out_ref[...] = acc / l


def _attn_sample(q, k_new, v_new, cache_k, cache_v, page_table, layer, *, n_pool, batch, dec_seq,
                 npg=8):
    n_pages = page_table.shape[1]
    prow = cache_k.shape[1]
    page = prow // N_KV_HEADS
    assert (n_pages * page) % MOBA_BLOCK == 0 and dec_seq <= MOBA_BLOCK
    assert n_pages % npg == 0 and MOBA_BLOCK % page == 0 and npg % (MOBA_BLOCK // page) == 0
    assert SUBLANES == 2 * N_KV_HEADS and prow % SUBLANES == 0
    nchunk = n_pages // npg
    nblk = n_pages * page // MOBA_BLOCK
    nrow = N_HEADS * dec_seq
    qd = q.reshape(batch, dec_seq, N_HEADS, HEAD_DIM).transpose(0, 2, 1, 3)
    qd = qd.reshape(batch, nrow, HEAD_DIM)
    per_b = lambda b, pt: (b, 0, 0)
    new_rows = dec_seq * N_KV_HEADS
    grid_spec = pltpu.PrefetchScalarGridSpec(
        num_scalar_prefetch=1,
        grid=(batch,),
        in_specs=[
            pl.BlockSpec((None, nrow, HEAD_DIM), per_b),
            pl.BlockSpec((None, new_rows, HEAD_DIM), per_b),
            pl.BlockSpec((None, new_rows, HEAD_DIM), per_b),
            pl.BlockSpec(memory_space=pl.ANY),
            pl.BlockSpec(memory_space=pl.ANY),
        ],
        out_specs=pl.BlockSpec((None, nrow, HEAD_DIM), per_b),
        scratch_shapes=[
            pltpu.VMEM((n_pages, prow, HEAD_DIM), F32),
            pltpu.VMEM((n_pages, prow, HEAD_DIM), F32),
            pltpu.VMEM((nrow, n_pages * prow), F32),
            pltpu.VMEM((2 * nrow, n_pages * prow), BF16),
            pltpu.VMEM((nblk * SUBLANES, HEAD_DIM), F32),
            pltpu.SemaphoreType.DMA((nchunk,)),
            pltpu.SemaphoreType.DMA((nchunk,)),
        ],
    )
    o = pl.pallas_call(
        functools.partial(_attn_sample_kernel, base=layer * n_pool, npg=npg, dec_seq=dec_seq),
        grid_spec=grid_spec,
        out_shape=jax.ShapeDtypeStruct((batch, nrow, HEAD_DIM), F32),
        compiler_params=_cparams(("arbitrary",)),
        name="attn_sample",
    )(page_table, qd, k_new.reshape(batch, new_rows, HEAD_DIM),
      v_new.reshape(batch, new_rows, HEAD_DIM), cache_k, cache_v)
    o = o.reshape(batch, N_HEADS, dec_seq, HEAD_DIM).transpose(0, 2, 1, 3)
    return o.reshape(batch * dec_seq, N_HEADS * HEAD_DIM)


def _oproj_kernel(o_ref, h_ref, wo_ref, out_ref):
    out_ref[...] = h_ref[...] + _dot3(o_ref[...], wo_ref[...])


def _oproj(o, h, wo, layer):
    n, d = h.shape
    full = pl.BlockSpec((n, d), lambda i: (0, 0))
    return pl.pallas_call(
        _oproj_kernel,
        grid=(1,),
        in_specs=[pl.BlockSpec(o.shape, lambda i: (0, 0)), full,
                  _layer_spec(wo.shape[1:], layer)],
        out_specs=full,
        out_shape=jax.ShapeDtypeStruct(h.shape, F32),
        compiler_params=_cparams(("arbitrary",)),
        name="oproj_sample",
    )(o, h, wo)


def _route(logits, lane):
    first_expert_lane = N_EXPERT_GROUPS
    is_group = lane < N_EXPERT_GROUPS
    gl = jnp.where(is_group, logits, NEG_INF)
    ge = jnp.where(is_group, jnp.exp(gl - jnp.max(gl, axis=-1, keepdims=True)), 0.0)
    gprob = ge / jnp.sum(ge, axis=-1, keepdims=True)
    g_w = jnp.max(gprob, axis=-1, keepdims=True)
    g_idx = jnp.min(jnp.where(is_group & (gprob == g_w), lane, LANES), axis=-1, keepdims=True)
    lo = first_expert_lane + g_idx * EXPERTS_PER_GROUP
    in_group = (lane >= lo) & (lane < lo + EXPERTS_PER_GROUP)
    el = jnp.where(in_group, logits, NEG_INF)
    ee = jnp.where(in_group, jnp.exp(el - jnp.max(el, axis=-1, keepdims=True)), 0.0)
    eprob = jnp.where(in_group, ee / jnp.sum(ee, axis=-1, keepdims=True), -1.0)
    p1 = jnp.max(eprob, axis=-1, keepdims=True)
    i1 = jnp.min(jnp.where(eprob == p1, lane, LANES), axis=-1, keepdims=True)
    rest = jnp.where(lane == i1, -1.0, eprob)
    p2 = jnp.max(rest, axis=-1, keepdims=True)
    i2 = jnp.min(jnp.where(rest == p2, lane, LANES), axis=-1, keepdims=True)
    denom = p1 + p2
    w1 = p1 / denom * g_w
    w2 = p2 / denom * g_w
    return g_idx, jnp.where(lane == i1, w1, 0.0) + jnp.where(lane == i2, w2, 0.0)


def _moe_kernel(h_ref, n2_ref, wr_ref, br_ref, wg_ref, wu_ref, wd_ref, out_ref,
                xn_scr, gates_scr, acc_scr, hid_scr, *, precise):
    g = pl.program_id(1)
    ngroups = pl.num_programs(1)
    tm = h_ref.shape[0]
    de = wg_ref.shape[2]
    lane = lax.broadcasted_iota(jnp.int32, (tm, LANES), 1)
    first_expert_lane = N_EXPERT_GROUPS

    @pl.when(g == 0)
    def _():
        x = h_ref[...]
        xn = _rms(x, n2_ref[...])
        xn_scr[...] = xn.astype(xn_scr.dtype)
        acc_scr[...] = x
        _, gates_scr[...] = _route(_mm(xn, wr_ref[...], precise) + br_ref[...], lane)

    xb = xn_scr[...]
    gates = gates_scr[...]
    for e in range(EXPERTS_PER_GROUP):
        a = _mm(xb, wg_ref[e], precise)
        b = _mm(xb, wu_ref[e], precise)
        tgt = first_expert_lane + g * EXPERTS_PER_GROUP + e
        gate = jnp.sum(jnp.where(lane == tgt, gates, 0.0), axis=-1, keepdims=True)
        hid_scr[:, e * de:(e + 1) * de] = (_silu(a) * b * gate).astype(hid_scr.dtype)
    acc_scr[...] += _mm(hid_scr[...], wd_ref[...], precise)

    @pl.when(g == ngroups - 1)
    def _():
        out_ref[...] = acc_scr[...]


def _moe(h, n2, wr, br, wg, wu, wd, layer, *, tm, precise):
    n, d = h.shape
    de = wg.shape[3]
    epg = EXPERTS_PER_GROUP
    const = lambda i, g: (0, 0)
    group = lambda i, g: (layer, g, 0, 0)
    wd = wd.reshape(wd.shape[0], N_EXPERT_GROUPS, epg * de, d)
    work = F32 if precise else BF16
    return pl.pallas_call(
        functools.partial(_moe_kernel, precise=precise),
        grid=(n // tm, N_EXPERT_GROUPS),
        in_specs=[
            pl.BlockSpec((tm, d), lambda i, g: (i, 0)),
            pl.BlockSpec((1, d), const),
            pl.BlockSpec((d, LANES), const),
            pl.BlockSpec((1, LANES), const),
            pl.BlockSpec((None, epg, d, de), group),
            pl.BlockSpec((None, epg, d, de), group),
            pl.BlockSpec((None, None, epg * de, d), group),
        ],
        out_specs=pl.BlockSpec((tm, d), lambda i, g: (i, 0)),
        out_shape=jax.ShapeDtypeStruct((n, d), F32),
        scratch_shapes=[
            pltpu.VMEM((tm, d), work),
            pltpu.VMEM((tm, LANES), F32),
            pltpu.VMEM((tm, d), F32),
            pltpu.VMEM((tm, epg * de), work),
        ],
        compiler_params=_cparams(("parallel", "arbitrary")),
        name="moe_sample" if precise else "moe",
    )(h, n2, wr, br, wg, wu, wd)


def kernel(x_prompt, x_sample, cache_k, cache_v, page_table, norm1, norm2, a_w_in, a_g_v, a_w_s, a_b_s, a_w_out, b_w_qkv, b_q_norm, b_k_norm, b_w_o, moe_w_group, moe_b_group, moe_w_router, moe_b_router, moe_w_gate, moe_w_up, moe_w_down):
    bp, tp, d = x_prompt.shape
    bs, ts, _ = x_sample.shape
    depth = norm1.shape[0]
    n_attn, n_pool, page = cache_k.shape[0], cache_k.shape[1], cache_k.shape[2]
    past_len = page_table.shape[1] * page
    assert CHUNK % ts == 0 and (bs * ts) % CHUNK == 0 and tp % MOBA_BLOCK == 0

    hp = x_prompt.reshape(bp * tp, d)
    hs = x_sample.reshape(bs * ts, d)
    ck = cache_k.reshape(n_attn * n_pool, page * N_KV_HEADS, HEAD_DIM)
    cv = cache_v.reshape(n_attn * n_pool, page * N_KV_HEADS, HEAD_DIM)
    tab_p = _rope_tables(jnp.arange(tp, dtype=jnp.int32))
    pos_s = past_len + jnp.arange(ts, dtype=jnp.int32)
    tab_s = tuple(jnp.tile(t, (bs, 1)) for t in _rope_tables(pos_s))

    pad = LANES - N_EXPERT_GROUPS - N_EXPERTS
    wr_all = jnp.concatenate(
        [moe_w_group, moe_w_router, jnp.zeros((depth, d, pad), F32)], axis=2)
    br_all = jnp.concatenate(
        [moe_b_group, moe_b_router, jnp.zeros((depth, pad), F32)], axis=1)[:, None, :]

    w_in_bf, w_out_bf = a_w_in.astype(BF16), a_w_out.astype(BF16)
    w_qkv_bf, w_o_bf = b_w_qkv.astype(BF16), b_w_o.astype(BF16)
    wg_bf, wu_bf, wd_bf = moe_w_gate.astype(BF16), moe_w_up.astype(BF16), moe_w_down.astype(BF16)

    new_k_p, new_v_p, new_k_s, new_v_s, new_chunk_v = [], [], [], [], []
    reps = CHUNK // ts
    for i in range(depth):
        j = i // 2
        n1 = norm1[i][None, :]
        if i % 2 == 0:
            g_v = a_g_v[j][None, :]
            hp = _mixer_a(hp, n1, w_in_bf, g_v, a_w_s[j], a_b_s[j][:, :, None], w_out_bf, j,
                          tm=1024, seq_len=CHUNK, emit_v=False, precise=False, nsplit=2)
            wmix_s = jnp.tile(a_w_s[j][:, :ts, :ts], (1, reps, reps))
            bs_s = jnp.tile(a_b_s[j][:, :ts], (1, reps))[:, :, None]
            hs, v_rows = _mixer_a(hs, n1, a_w_in, g_v, wmix_s, bs_s, a_w_out, j,
                                  tm=CHUNK, seq_len=ts, emit_v=True, precise=True)
            new_chunk_v.append(v_rows.reshape(bs, ts, -1))
        else:
            qn = b_q_norm[j][None, :]
            kn = b_k_norm[j][None, :]
            qp, kp, vp = _qkv(hp, n1, w_qkv_bf, j, qn, kn, tab_p, tm=1024, precise=False, nsplit=2)
            qs, ks_new, vs_new = _qkv(hs, n1, b_w_qkv, j, qn, kn, tab_s, tm=bs * ts, precise=True)
            hp = _attn_prompt(qp, kp, vp, hp, w_o_bf, j, batch=bp, seq=tp)
            o_s = _attn_sample(qs, ks_new, vs_new, ck, cv, page_table, j, n_pool=n_pool,
                               batch=bs, dec_seq=ts)
            hs = _oproj(o_s, hs, b_w_o, j)
            new_k_p.append(kp.reshape(bp, tp, N_KV_HEADS, HEAD_DIM))
            new_v_p.append(vp.reshape(bp, tp, N_KV_HEADS, HEAD_DIM))
            new_k_s.append(ks_new.reshape(bs, ts, N_KV_HEADS, HEAD_DIM))
            new_v_s.append(vs_new.reshape(bs, ts, N_KV_HEADS, HEAD_DIM))
        n2 = norm2[i][None, :]
        hp = _moe(hp, n2, wr_all[i], br_all[i], wg_bf, wu_bf, wd_bf, i,
                  tm=min(1024, bp * tp), precise=False)
        hs = _moe(hs, n2, wr_all[i], br_all[i], moe_w_gate, moe_w_up, moe_w_down, i,
                  tm=bs * ts, precise=True)
    return (hp.reshape(bp, tp, d), hs.reshape(bs, ts, d),
            jnp.stack(new_k_p), jnp.stack(new_v_p), jnp.stack(new_k_s), jnp.stack(new_v_s),
            jnp.stack(new_chunk_v))
```

```python
import functools

import jax
import jax.numpy as jnp
from jax import lax
from jax.experimental import pallas as pl
from jax.experimental.pallas import tpu as pltpu

F32 = jnp.float32
BF16 = jnp.bfloat16

CHUNK = 128
CHUNK_GROUPS = 4
HEAD_DIM = 128
N_HEADS = 8
N_KV_HEADS = 4
KV_GROUP = N_HEADS // N_KV_HEADS
ROT_DIM = HEAD_DIM // 4
ROPE_THETA = 500000.0
MOBA_BLOCK = 256
MOBA_TOP_K = 3
N_EXPERT_GROUPS = 4
EXPERTS_PER_GROUP = 4
N_EXPERTS = N_EXPERT_GROUPS * EXPERTS_PER_GROUP
NORM_EPS = 1e-6
NEG_INF = -1e30
LOG2E = 1.4426950408889634
SPAN_BLOCKS = 2

SUBLANES = 8
LANES = 128
VMEM_LIMIT_BYTES = 56 * 1024 * 1024

_NT = (((1,), (1,)), ((), ()))


def _dot(a, b):
    return jnp.dot(a, b, preferred_element_type=F32)


def _dot_nt(a, b):
    return lax.dot_general(a, b, _NT, preferred_element_type=F32)


def _split_bf16(a):
    hi = a.astype(BF16)
    lo = (a - hi.astype(F32)).astype(BF16)
    return hi, lo


def _dot3(a, b, nt=False):
    d = _dot_nt if nt else _dot
    ah, al = _split_bf16(a)
    bh, bl = _split_bf16(b)
    return d(ah, bh) + (d(ah, bl) + d(al, bh))


def _mm(x, w, precise):
    if precise:
        return _dot3(x, w)
    return _dot(x.astype(BF16), w.astype(BF16))


def _rms(x, g):
    ms = jnp.mean(x * x, axis=-1, keepdims=True)
    return x * lax.rsqrt(ms + NORM_EPS) * g


def _gelu_tanh(x):
    c = 0.7978845608028654
    return 0.5 * x * (1.0 + jnp.tanh(c * (x + 0.044715 * (x * x * x))))


def _silu(x):
    return x * (1.0 / (1.0 + jnp.exp(-x)))


def _cparams(sem):
    return pltpu.CompilerParams(dimension_semantics=sem, vmem_limit_bytes=VMEM_LIMIT_BYTES)


def _layer_spec(shape, layer):
    zeros = (0,) * len(shape)
    return pl.BlockSpec((None,) + tuple(shape), lambda *_: (layer,) + zeros)


def _mixer_a_kernel(h_ref, n1_ref, win_ref, gv_ref, wmix_ref, bs_ref, wout_ref, *rest,
                    seq_len, emit_v, precise, nsplit):
    if emit_v:
        out_ref, v_ref, us_scr = rest
    else:
        out_ref, us_scr = rest
    width = gv_ref.shape[1]
    gdim = width // CHUNK_GROUPS
    sub = h_ref.shape[0] // nsplit
    row = lax.broadcasted_iota(jnp.int32, (CHUNK, CHUNK), 0)
    col = lax.broadcasted_iota(jnp.int32, (CHUNK, CHUNK), 1)
    mask = col <= row
    if seq_len < CHUNK:
        mask = mask & ((row // seq_len) == (col // seq_len))
    wms = [jnp.where(mask, wmix_ref[g], 0.0) for g in range(CHUNK_GROUPS)]
    for sl in range(nsplit):
        srows = slice(sl * sub, (sl + 1) * sub)
        x = h_ref[srows, :]
        xn = _rms(x, n1_ref[...])
        z = _gelu_tanh(_mm(xn, win_ref[...], precise))
        u = z[:, :width]
        vn = _rms(z[:, width:], gv_ref[...])
        if emit_v:
            v_ref[srows, :] = vn
        if not precise:
            vn = vn.astype(BF16)
        for g in range(CHUNK_GROUPS):
            bias = bs_ref[g]
            for c in range(sub // CHUNK):
                rows = slice(c * CHUNK, (c + 1) * CHUNK)
                cols = slice(g * gdim, (g + 1) * gdim)
                s = _mm(wms[g], vn[rows, cols], precise) + bias
                us_scr[sl * sub + c * CHUNK:sl * sub + (c + 1) * CHUNK, cols] = (
                    (u[rows, cols] * s).astype(us_scr.dtype))
        out_ref[srows, :] = x + _mm(us_scr[srows, :], wout_ref[...], precise)


def _mixer_a(h, n1, w_in, g_v, wmix, bs, w_out, layer, *, tm, seq_len, emit_v, precise, nsplit=1):
    n, d = h.shape
    width = g_v.shape[1]
    const = lambda i: (0, 0)
    out_shape = [jax.ShapeDtypeStruct((n, d), F32)]
    out_specs = [pl.BlockSpec((tm, d), lambda i: (i, 0))]
    if emit_v:
        out_shape.append(jax.ShapeDtypeStruct((n, width), F32))
        out_specs.append(pl.BlockSpec((tm, width), lambda i: (i, 0)))
    res = pl.pallas_call(
        functools.partial(_mixer_a_kernel, seq_len=seq_len, emit_v=emit_v, precise=precise,
                          nsplit=nsplit),
        grid=(n // tm,),
        in_specs=[
            pl.BlockSpec((tm, d), lambda i: (i, 0)),
            pl.BlockSpec((1, d), const),
            _layer_spec((d, 2 * width), layer),
            pl.BlockSpec((1, width), const),
            pl.BlockSpec((CHUNK_GROUPS, CHUNK, CHUNK), lambda i: (0, 0, 0)),
            pl.BlockSpec((CHUNK_GROUPS, CHUNK, 1), lambda i: (0, 0, 0)),
            _layer_spec((width, d), layer),
        ],
        out_specs=out_specs,
        out_shape=out_shape,
        scratch_shapes=[pltpu.VMEM((tm, width), F32 if precise else BF16)],
        compiler_params=_cparams(("parallel",)),
        name="mixer_a_sample" if emit_v else "mixer_a",
    )(h, n1, w_in, g_v, wmix, bs, w_out)
    return res if emit_v else res[0]


def _qkv_kernel(h_ref, n1_ref, w_ref, qn_ref, kn_ref, cos_ref, sa_ref, sb_ref,
                q_ref, k_ref, v_ref, *, precise, nsplit):
    nq = q_ref.shape[1]
    nk = N_KV_HEADS * HEAD_DIM
    sub = h_ref.shape[0] // nsplit
    for s in range(nsplit):
        rows = slice(s * sub, (s + 1) * sub)
        xn = _rms(h_ref[rows, :], n1_ref[...])
        qkv = _mm(xn, w_ref[...], precise)
        cos = cos_ref[rows, :]
        sa = sa_ref[rows, :]
        sb = sb_ref[rows, :]

        def head(t, g, cos=cos, sa=sa, sb=sb):
            t = _rms(t, g)
            return (t * cos + pltpu.roll(t, HEAD_DIM - ROT_DIM // 2, 1) * sa
                    + pltpu.roll(t, ROT_DIM // 2, 1) * sb)

        for hd in range(nq // HEAD_DIM):
            cols = slice(hd * HEAD_DIM, (hd + 1) * HEAD_DIM)
            q_ref[rows, cols] = head(qkv[:, cols], qn_ref[...])
        for hd in range(N_KV_HEADS):
            kv_rows = pl.ds(s * sub * N_KV_HEADS + hd, sub, stride=N_KV_HEADS)
            k_ref[kv_rows, :] = head(qkv[:, nq + hd * HEAD_DIM: nq + (hd + 1) * HEAD_DIM],
                                     kn_ref[...])
            v_ref[kv_rows, :] = qkv[:, nq + nk + hd * HEAD_DIM: nq + nk + (hd + 1) * HEAD_DIM]


def _rope_tables(pos):
    half = ROT_DIM // 2
    inv_freq = jnp.power(ROPE_THETA, -jnp.arange(half, dtype=F32) * 2.0 / ROT_DIM)
    ang = pos.astype(F32)[:, None] * inv_freq[None, :]
    cos = jnp.cos(ang)
    sin = jnp.sin(ang)
    t = pos.shape[0]
    ones = jnp.ones((t, HEAD_DIM - ROT_DIM), F32)
    zeros = jnp.zeros((t, HEAD_DIM - ROT_DIM), F32)
    z16 = jnp.zeros((t, half), F32)
    cos_t = jnp.concatenate([cos, cos, ones], axis=1)
    sa_t = jnp.concatenate([-sin, z16, zeros], axis=1)
    sb_t = jnp.concatenate([z16, sin, zeros], axis=1)
    return cos_t, sa_t, sb_t


def _qkv(h, n1, w, layer, qn, kn, tables, *, tm, precise, nsplit=1):
    n, d = h.shape
    nq = N_HEADS * HEAD_DIM
    nk = N_KV_HEADS * HEAD_DIM
    t = tables[0].shape[0]
    tiles_per_seq = t // tm
    const = lambda i: (0, 0)
    tab = pl.BlockSpec((tm, HEAD_DIM), lambda i: (i % tiles_per_seq, 0))
    return pl.pallas_call(
        functools.partial(_qkv_kernel, precise=precise, nsplit=nsplit),
        grid=(n // tm,),
        in_specs=[
            pl.BlockSpec((tm, d), lambda i: (i, 0)),
            pl.BlockSpec((1, d), const),
            _layer_spec((d, nq + 2 * nk), layer),
            pl.BlockSpec((1, HEAD_DIM), const),
            pl.BlockSpec((1, HEAD_DIM), const),
            tab, tab, tab,
        ],
        out_specs=[
            pl.BlockSpec((tm, nq), lambda i: (i, 0)),
            pl.BlockSpec((tm * N_KV_HEADS, HEAD_DIM), lambda i: (i, 0)),
            pl.BlockSpec((tm * N_KV_HEADS, HEAD_DIM), lambda i: (i, 0)),
        ],
        out_shape=[
            jax.ShapeDtypeStruct((n, nq), F32),
            jax.ShapeDtypeStruct((n * N_KV_HEADS, HEAD_DIM), F32),
            jax.ShapeDtypeStruct((n * N_KV_HEADS, HEAD_DIM), F32),
        ],
        compiler_params=_cparams(("parallel",)),
        name="qkv_sample" if precise else "qkv",
    )(h, n1, w, qn, kn, *tables)


def _attn_prompt_kernel(q_ref, k_ref, v_ref, h_ref, wo_ref, out_ref,
                        kbf, vtb, kmean, s_scr, p_scr, o_scr):
    blk = MOBA_BLOCK
    nblk = kbf.shape[0] // blk
    own = pl.program_id(1)
    scale = HEAD_DIM ** -0.5

    @pl.when(own == 0)
    def _():
        for n in range(nblk):
            rows = slice(n * blk, (n + 1) * blk)
            for kh in range(N_KV_HEADS):
                kcols = slice(kh * HEAD_DIM, (kh + 1) * HEAD_DIM)
                src_rows = pl.ds(n * blk * N_KV_HEADS + kh, blk, stride=N_KV_HEADS)
                kk = k_ref[src_rows, :]
                kbf[rows, kcols] = kk.astype(BF16)
                km_hi, km_lo = _split_bf16(jnp.mean(kk, axis=0, keepdims=True))
                kmean[n:n + 1, kcols] = km_hi.astype(F32)
                kmean[nblk + n:nblk + n + 1, kcols] = km_lo.astype(F32)
                vtb[kh, :HEAD_DIM, rows] = v_ref[src_rows, :].T.astype(BF16)
        for kh in range(N_KV_HEADS):
            vtb[kh, HEAD_DIM:, :] = jnp.ones((vtb.shape[1] - HEAD_DIM, vtb.shape[2]), BF16)

    cols = KV_GROUP * blk
    nidx = lax.broadcasted_iota(jnp.int32, (nblk, cols), 0)
    kq_diff = (lax.broadcasted_iota(jnp.int32, (blk, cols), 0)
               - lax.broadcasted_iota(jnp.int32, (blk, cols), 1) % blk)

    for kh in range(N_KV_HEADS):
        kcols = slice(kh * HEAD_DIM, (kh + 1) * HEAD_DIM)
        q2 = jnp.concatenate(
            [q_ref[:, (KV_GROUP * kh + i) * HEAD_DIM:(KV_GROUP * kh + i + 1) * HEAD_DIM]
             for i in range(KV_GROUP)], axis=0)
        q2s = (q2 * (scale * LOG2E)).astype(BF16)

        g2 = _dot_nt(kmean[:, kcols].astype(BF16), q2s)
        g_t = g2[:nblk] + g2[nblk:]
        rank = jnp.zeros_like(g_t)
        for m in range(nblk - 1):
            gm = g_t[m:m + 1, :]
            beats = (gm > g_t) | ((gm == g_t) & (nidx > m))
            inc = jnp.where(m < own, 1.0, 0.0)
            rank = rank + jnp.where(beats, inc, 0.0)
        sel_t = jnp.where(nidx == own, 1.0,
                          jnp.where((nidx < own) & (rank < float(MOBA_TOP_K)), 1.0, 0.0))

        def attend(nb, kh=kh, kcols=kcols, q2s=q2s, sel_t=sel_t):
            m = jnp.full((1, cols), NEG_INF, F32)
            for n in range(nb):
                rows = slice(n * blk, (n + 1) * blk)
                s = _dot_nt(kbf[rows, kcols], q2s)
                s = jnp.where(sel_t[n:n + 1, :] > 0.5, s, NEG_INF)
                if n >= nb - SPAN_BLOCKS:
                    s = jnp.where(kq_diff <= (own - n) * blk, s, NEG_INF)
                s_scr[rows, :] = s
                m = jnp.maximum(m, jnp.max(s, axis=0, keepdims=True))
            for n in range(nb):
                rows = slice(n * blk, (n + 1) * blk)
                p_scr[rows, :] = jnp.exp2(s_scr[rows, :] - m).astype(BF16)
            acc = _dot(vtb[kh, :, :nb * blk], p_scr[:nb * blk, :])
            o = (acc[:HEAD_DIM] / acc[HEAD_DIM:HEAD_DIM + 1]).T.astype(BF16)
            for i in range(KV_GROUP):
                hd = KV_GROUP * kh + i
                o_scr[:, hd * HEAD_DIM:(hd + 1) * HEAD_DIM] = o[i * blk:(i + 1) * blk]

        for nb in range(SPAN_BLOCKS, nblk + SPAN_BLOCKS, SPAN_BLOCKS):
            pl.when((own >= nb - SPAN_BLOCKS) & (own < nb))(functools.partial(attend, min(nb, nblk)))

    out_ref[...] = h_ref[...] + _dot(o_scr[...], wo_ref[...])


def _attn_prompt(q, k, v, h, wo, layer, *, batch, seq):
    blk = MOBA_BLOCK
    nqb = seq // blk
    d = h.shape[1]
    nq = q.shape[1]
    nk = N_KV_HEADS * HEAD_DIM
    cols = KV_GROUP * blk
    return pl.pallas_call(
        _attn_prompt_kernel,
        grid=(batch, nqb),
        in_specs=[
            pl.BlockSpec((blk, nq), lambda b, i: (b * nqb + i, 0)),
            pl.BlockSpec((seq * N_KV_HEADS, HEAD_DIM), lambda b, i: (b, 0)),
            pl.BlockSpec((seq * N_KV_HEADS, HEAD_DIM), lambda b, i: (b, 0)),
            pl.BlockSpec((blk, d), lambda b, i: (b * nqb + i, 0)),
            _layer_spec((nq, d), layer),
        ],
        out_specs=pl.BlockSpec((blk, d), lambda b, i: (b * nqb + i, 0)),
        out_shape=jax.ShapeDtypeStruct(h.shape, F32),
        scratch_shapes=[
            pltpu.VMEM((seq, nk), BF16),
            pltpu.VMEM((N_KV_HEADS, HEAD_DIM + 2 * SUBLANES, seq), BF16),
            pltpu.VMEM((2 * nqb, nk), F32),
            pltpu.VMEM((seq, cols), F32),
            pltpu.VMEM((seq, cols), BF16),
            pltpu.VMEM((blk, nq), BF16),
        ],
        compiler_params=_cparams(("parallel", "arbitrary")),
        name="attn_prompt",
    )(q, k, v, h, wo)


def _attn_sample_kernel(pt_ref, qd_ref, knew_ref, vnew_ref, ck_hbm, cv_hbm, out_ref,
                        kbuf, vbuf, s_scr, p_scr, kmean_scr, ksem, vsem,
                        *, base, npg, dec_seq):
    b = pl.program_id(0)
    nbatch = pl.num_programs(0)
    n_pages, prow, _ = kbuf.shape
    nchunk = n_pages // npg
    ccols = npg * prow
    bcols = MOBA_BLOCK * N_KV_HEADS
    pages_per_blk = bcols // prow
    blk_per_chunk = npg // pages_per_blk
    nrow = qd_ref.shape[0]
    nl = kmean_scr.shape[0]
    nblk = nl // SUBLANES
    scale = HEAD_DIM ** -0.5

    def page_copy(hbm, buf, sem, bb, pg):
        return pltpu.make_async_copy(hbm.at[base + pt_ref[bb, pg]], buf.at[pg], sem.at[pg // npg])

    def fetch(hbm, buf, sem, bb):
        for pg in range(n_pages):
            page_copy(hbm, buf, sem, bb, pg).start()

    def wait_chunk(hbm, buf, sem, c):
        for pg in range(c * npg, (c + 1) * npg):
            page_copy(hbm, buf, sem, b, pg).wait()

    @pl.when(b == 0)
    def _():
        fetch(ck_hbm, kbuf, ksem, 0)
        fetch(cv_hbm, vbuf, vsem, 0)

    qd = qd_ref[...]
    q_hi, q_lo = _split_bf16(qd)
    q_hl = jnp.concatenate([q_hi, q_lo], axis=0)

    for c in range(nchunk):
        wait_chunk(ck_hbm, kbuf, ksem, c)
        kf = [kbuf[c * npg + i] for i in range(npg)]
        for i in range(blk_per_chunk):
            tot = kf[pages_per_blk * i]
            for j in range(1, pages_per_blk):
                tot = tot + kf[pages_per_blk * i + j]
            part = jnp.sum(tot.reshape(prow // SUBLANES, SUBLANES, HEAD_DIM), axis=0)
            part = (part + pltpu.roll(part, N_KV_HEADS, 0)) * (1.0 / MOBA_BLOCK)
            row0 = (c * blk_per_chunk + i) * SUBLANES
            kmean_scr[row0:row0 + SUBLANES, :] = part
        k_hi, k_lo = _split_bf16(jnp.concatenate(kf, axis=0))
        s2 = _dot_nt(q_hl, k_hi)
        s_scr[:, c * ccols:(c + 1) * ccols] = (s2[:nrow] + s2[nrow:]) + _dot_nt(q_hi, k_lo)

    @pl.when(b + 1 < nbatch)
    def _():
        fetch(ck_hbm, kbuf, ksem, b + 1)

    g = _dot3(qd, kmean_scr[...], nt=True)
    lane = lax.broadcasted_iota(jnp.int32, (nrow, nl), 1)
    kv_r = (lax.broadcasted_iota(jnp.int32, (nrow, nl), 0) // dec_seq) // KV_GROUP
    rowsel = [jnp.where(kv_r == kh, 1.0, 0.0) for kh in range(N_KV_HEADS)]
    rank = jnp.zeros_like(g)
    for m in range(nblk):
        for kh in range(N_KV_HEADS):
            pos = m * SUBLANES + kh
            col = g[:, pos:pos + 1]
            beats = (col > g) | ((col == g) & (lane > pos))
            rank = rank + jnp.where(beats, rowsel[kh], 0.0)
    sel = jnp.where((lane % SUBLANES) == kv_r,
                    jnp.where(rank < float(MOBA_TOP_K), 1.0, 0.0), 0.0)
    blkid = lane // SUBLANES

    kv_c = (lax.broadcasted_iota(jnp.int32, (nrow, 1), 0) // dec_seq) // KV_GROUP
    t_row = lax.broadcasted_iota(jnp.int32, (nrow, 1), 0) % dec_seq

    def rows_for(x, t):
        out = jnp.zeros((nrow, HEAD_DIM), F32)
        for kh in range(N_KV_HEADS):
            r = t * N_KV_HEADS + kh
            out = out + jnp.where(kv_c == kh, x[r:r + 1, :], 0.0)
        return out

    knew = knew_ref[...]
    vnew = vnew_ref[...]
    s_own = [jnp.sum(qd * rows_for(knew, t), axis=-1, keepdims=True) * scale
             for t in range(dec_seq)]
    ok_own = [t_row >= t for t in range(dec_seq)]

    kh_match = (lax.broadcasted_iota(jnp.int32, (nrow, bcols), 1) % N_KV_HEADS) == (
        (lax.broadcasted_iota(jnp.int32, (nrow, bcols), 0) // dec_seq) // KV_GROUP)
    m = jnp.full((nrow, 1), NEG_INF, F32)
    for t in range(dec_seq):
        m = jnp.maximum(m, jnp.where(ok_own[t], s_own[t], NEG_INF))
    for n in range(nblk):
        cs = slice(n * bcols, (n + 1) * bcols)
        pick = jnp.max(jnp.where(blkid == n, sel, 0.0), axis=1, keepdims=True)
        sc = jnp.where(pick > 0.5, s_scr[:, cs] * scale, NEG_INF)
        sc = jnp.where(kh_match, sc, NEG_INF)
        s_scr[:, cs] = sc
        m = jnp.maximum(m, jnp.max(sc, axis=-1, keepdims=True))
    l = jnp.zeros((nrow, 1), F32)
    for n in range(nblk):
        cs = slice(n * bcols, (n + 1) * bcols)
        p = jnp.exp(s_scr[:, cs] - m)
        l = l + jnp.sum(p, axis=-1, keepdims=True)
        p_hi, p_lo = _split_bf16(p)
        p_scr[:nrow, cs] = p_hi
        p_scr[nrow:, cs] = p_lo
    acc = jnp.zeros((nrow, HEAD_DIM), F32)
    for t in range(dec_seq):
        p_t = jnp.where(ok_own[t], jnp.exp(s_own[t] - m), 0.0)
        l = l + p_t
        acc = acc + p_t * rows_for(vnew, t)

    for c in range(nchunk):
        wait_chunk(cv_hbm, vbuf, vsem, c)
        vb = jnp.concatenate([vbuf[c * npg + i].astype(BF16) for i in range(npg)], axis=0)
        pv = _dot(p_scr[:, c * ccols:(c + 1) * ccols], vb)
        acc = acc + (pv[:nrow] + pv[nrow:])

    @pl.when(b + 1 < nbatch)
    def _():
        fetch(cv_hbm, vbuf, vsem, b + 1)

    out_ref[...] = acc / l


def _attn_sample(q, k_new, v_new, cache_k, cache_v, page_table, layer, *, n_pool, batch, dec_seq,
                 npg=8):
    n_pages = page_table.shape[1]
    prow = cache_k.shape[1]
    page = prow // N_KV_HEADS
    assert (n_pages * page) % MOBA_BLOCK == 0 and dec_seq <= MOBA_BLOCK
    assert n_pages % npg == 0 and MOBA_BLOCK % page == 0 and npg % (MOBA_BLOCK // page) == 0
    assert SUBLANES == 2 * N_KV_HEADS and prow % SUBLANES == 0
    nchunk = n_pages // npg
    nblk = n_pages * page // MOBA_BLOCK
    nrow = N_HEADS * dec_seq
    qd = q.reshape(batch, dec_seq, N_HEADS, HEAD_DIM).transpose(0, 2, 1, 3)
    qd = qd.reshape(batch, nrow, HEAD_DIM)
    per_b = lambda b, pt: (b, 0, 0)
    new_rows = dec_seq * N_KV_HEADS
    grid_spec = pltpu.PrefetchScalarGridSpec(
        num_scalar_prefetch=1,
        grid=(batch,),
        in_specs=[
            pl.BlockSpec((None, nrow, HEAD_DIM), per_b),
            pl.BlockSpec((None, new_rows, HEAD_DIM), per_b),
            pl.BlockSpec((None, new_rows, HEAD_DIM), per_b),
            pl.BlockSpec(memory_space=pl.ANY),
            pl.BlockSpec(memory_space=pl.ANY),
        ],
        out_specs=pl.BlockSpec((None, nrow, HEAD_DIM), per_b),
        scratch_shapes=[
            pltpu.VMEM((n_pages, prow, HEAD_DIM), F32),
            pltpu.VMEM((n_pages, prow, HEAD_DIM), F32),
            pltpu.VMEM((nrow, n_pages * prow), F32),
            pltpu.VMEM((2 * nrow, n_pages * prow), BF16),
            pltpu.VMEM((nblk * SUBLANES, HEAD_DIM), F32),
            pltpu.SemaphoreType.DMA((nchunk,)),
            pltpu.SemaphoreType.DMA((nchunk,)),
        ],
    )
    o = pl.pallas_call(
        functools.partial(_attn_sample_kernel, base=layer * n_pool, npg=npg, dec_seq=dec_seq),
        grid_spec=grid_spec,
        out_shape=jax.ShapeDtypeStruct((batch, nrow, HEAD_DIM), F32),
        compiler_params=_cparams(("arbitrary",)),
        name="attn_sample",
    )(page_table, qd, k_new.reshape(batch, new_rows, HEAD_DIM),
      v_new.reshape(batch, new_rows, HEAD_DIM), cache_k, cache_v)
    o = o.reshape(batch, N_HEADS, dec_seq, HEAD_DIM).transpose(0, 2, 1, 3)
    return o.reshape(batch * dec_seq, N_HEADS * HEAD_DIM)


def _oproj_kernel(o_ref, h_ref, wo_ref, out_ref):
    out_ref[...] = h_ref[...] + _dot3(o_ref[...], wo_ref[...])


def _oproj(o, h, wo, layer):
    n, d = h.shape
    full = pl.BlockSpec((n, d), lambda i: (0, 0))
    return pl.pallas_call(
        _oproj_kernel,
        grid=(1,),
        in_specs=[pl.BlockSpec(o.shape, lambda i: (0, 0)), full,
                  _layer_spec(wo.shape[1:], layer)],
        out_specs=full,
        out_shape=jax.ShapeDtypeStruct(h.shape, F32),
        compiler_params=_cparams(("arbitrary",)),
        name="oproj_sample",
    )(o, h, wo)


def _route(logits, lane):
    first_expert_lane = N_EXPERT_GROUPS
    is_group = lane < N_EXPERT_GROUPS
    gl = jnp.where(is_group, logits, NEG_INF)
    ge = jnp.where(is_group, jnp.exp(gl - jnp.max(gl, axis=-1, keepdims=True)), 0.0)
    gprob = ge / jnp.sum(ge, axis=-1, keepdims=True)
    g_w = jnp.max(gprob, axis=-1, keepdims=True)
    g_idx = jnp.min(jnp.where(is_group & (gprob == g_w), lane, LANES), axis=-1, keepdims=True)
    lo = first_expert_lane + g_idx * EXPERTS_PER_GROUP
    in_group = (lane >= lo) & (lane < lo + EXPERTS_PER_GROUP)
    el = jnp.where(in_group, logits, NEG_INF)
    ee = jnp.where(in_group, jnp.exp(el - jnp.max(el, axis=-1, keepdims=True)), 0.0)
    eprob = jnp.where(in_group, ee / jnp.sum(ee, axis=-1, keepdims=True), -1.0)
    p1 = jnp.max(eprob, axis=-1, keepdims=True)
    i1 = jnp.min(jnp.where(eprob == p1, lane, LANES), axis=-1, keepdims=True)
    rest = jnp.where(lane == i1, -1.0, eprob)
    p2 = jnp.max(rest, axis=-1, keepdims=True)
    i2 = jnp.min(jnp.where(rest == p2, lane, LANES), axis=-1, keepdims=True)
    denom = p1 + p2
    w1 = p1 / denom * g_w
    w2 = p2 / denom * g_w
    return g_idx, jnp.where(lane == i1, w1, 0.0) + jnp.where(lane == i2, w2, 0.0)


def _moe_kernel(h_ref, n2_ref, wr_ref, br_ref, wg_ref, wu_ref, wd_ref, out_ref,
                xn_scr, gates_scr, acc_scr, hid_scr, *, precise):
    g = pl.program_id(1)
    ngroups = pl.num_programs(1)
    tm = h_ref.shape[0]
    de = wg_ref.shape[2]
    lane = lax.broadcasted_iota(jnp.int32, (tm, LANES), 1)
    first_expert_lane = N_EXPERT_GROUPS

    @pl.when(g == 0)
    def _():
        x = h_ref[...]
        xn = _rms(x, n2_ref[...])
        xn_scr[...] = xn.astype(xn_scr.dtype)
        acc_scr[...] = x
        _, gates_scr[...] = _route(_mm(xn, wr_ref[...], precise) + br_ref[...], lane)

    xb = xn_scr[...]
    gates = gates_scr[...]
    for e in range(EXPERTS_PER_GROUP):
        a = _mm(xb, wg_ref[e], precise)
        b = _mm(xb, wu_ref[e], precise)
        tgt = first_expert_lane + g * EXPERTS_PER_GROUP + e
        gate = jnp.sum(jnp.where(lane == tgt, gates, 0.0), axis=-1, keepdims=True)
        hid_scr[:, e * de:(e + 1) * de] = (_silu(a) * b * gate).astype(hid_scr.dtype)
    acc_scr[...] += _mm(hid_scr[...], wd_ref[...], precise)

    @pl.when(g == ngroups - 1)
    def _():
        out_ref[...] = acc_scr[...]


def _moe(h, n2, wr, br, wg, wu, wd, layer, *, tm, precise):
    n, d = h.shape
    de = wg.shape[3]
    epg = EXPERTS_PER_GROUP
    const = lambda i, g: (0, 0)
    group = lambda i, g: (layer, g, 0, 0)
    wd = wd.reshape(wd.shape[0], N_EXPERT_GROUPS, epg * de, d)
    work = F32 if precise else BF16
    return pl.pallas_call(
        functools.partial(_moe_kernel, precise=precise),
        grid=(n // tm, N_EXPERT_GROUPS),
        in_specs=[
            pl.BlockSpec((tm, d), lambda i, g: (i, 0)),
            pl.BlockSpec((1, d), const),
            pl.BlockSpec((d, LANES), const),
            pl.BlockSpec((1, LANES), const),
            pl.BlockSpec((None, epg, d, de), group),
            pl.BlockSpec((None, epg, d, de), group),
            pl.BlockSpec((None, None, epg * de, d), group),
        ],
        out_specs=pl.BlockSpec((tm, d), lambda i, g: (i, 0)),
        out_shape=jax.ShapeDtypeStruct((n, d), F32),
        scratch_shapes=[
            pltpu.VMEM((tm, d), work),
            pltpu.VMEM((tm, LANES), F32),
            pltpu.VMEM((tm, d), F32),
            pltpu.VMEM((tm, epg * de), work),
        ],
        compiler_params=_cparams(("parallel", "arbitrary")),
        name="moe_sample" if precise else "moe",
    )(h, n2, wr, br, wg, wu, wd)


def kernel(x_prompt, x_sample, cache_k, cache_v, page_table, norm1, norm2, a_w_in, a_g_v, a_w_s, a_b_s, a_w_out, b_w_qkv, b_q_norm, b_k_norm, b_w_o, moe_w_group, moe_b_group, moe_w_router, moe_b_router, moe_w_gate, moe_w_up, moe_w_down):
    bp, tp, d = x_prompt.shape
    bs, ts, _ = x_sample.shape
    depth = norm1.shape[0]
    n_attn, n_pool, page = cache_k.shape[0], cache_k.shape[1], cache_k.shape[2]
    past_len = page_table.shape[1] * page
    assert CHUNK % ts == 0 and (bs * ts) % CHUNK == 0 and tp % MOBA_BLOCK == 0

    hp = x_prompt.reshape(bp * tp, d)
    hs = x_sample.reshape(bs * ts, d)
    ck = cache_k.reshape(n_attn * n_pool, page * N_KV_HEADS, HEAD_DIM)
    cv = cache_v.reshape(n_attn * n_pool, page * N_KV_HEADS, HEAD_DIM)
    tab_p = _rope_tables(jnp.arange(tp, dtype=jnp.int32))
    pos_s = past_len + jnp.arange(ts, dtype=jnp.int32)
    tab_s = tuple(jnp.tile(t, (bs, 1)) for t in _rope_tables(pos_s))

    pad = LANES - N_EXPERT_GROUPS - N_EXPERTS
    wr_all = jnp.concatenate(
        [moe_w_group, moe_w_router, jnp.zeros((depth, d, pad), F32)], axis=2)
    br_all = jnp.concatenate(
        [moe_b_group, moe_b_router, jnp.zeros((depth, pad), F32)], axis=1)[:, None, :]

    w_in_bf, w_out_bf = a_w_in.astype(BF16), a_w_out.astype(BF16)
    w_qkv_bf, w_o_bf = b_w_qkv.astype(BF16), b_w_o.astype(BF16)
    wg_bf, wu_bf, wd_bf = moe_w_gate.astype(BF16), moe_w_up.astype(BF16), moe_w_down.astype(BF16)

    new_k_p, new_v_p, new_k_s, new_v_s, new_chunk_v = [], [], [], [], []
    reps = CHUNK // ts
    for i in range(depth):
        j = i // 2
        n1 = norm1[i][None, :]
        if i % 2 == 0:
            g_v = a_g_v[j][None, :]
            hp = _mixer_a(hp, n1, w_in_bf, g_v, a_w_s[j], a_b_s[j][:, :, None], w_out_bf, j,
                          tm=1024, seq_len=CHUNK, emit_v=False, precise=False, nsplit=2)
            wmix_s = jnp.tile(a_w_s[j][:, :ts, :ts], (1, reps, reps))
            bs_s = jnp.tile(a_b_s[j][:, :ts], (1, reps))[:, :, None]
            hs, v_rows = _mixer_a(hs, n1, a_w_in, g_v, wmix_s, bs_s, a_w_out, j,
                                  tm=CHUNK, seq_len=ts, emit_v=True, precise=True)
            new_chunk_v.append(v_rows.reshape(bs, ts, -1))
        else:
            qn = b_q_norm[j][None, :]
            kn = b_k_norm[j][None, :]
            qp, kp, vp = _qkv(hp, n1, w_qkv_bf, j, qn, kn, tab_p, tm=1024, precise=False, nsplit=2)
            qs, ks_new, vs_new = _qkv(hs, n1, b_w_qkv, j, qn, kn, tab_s, tm=bs * ts, precise=True)
            hp = _attn_prompt(qp, kp, vp, hp, w_o_bf, j, batch=bp, seq=tp)
            o_s = _attn_sample(qs, ks_new, vs_new, ck, cv, page_table, j, n_pool=n_pool,
                               batch=bs, dec_seq=ts)
            hs = _oproj(o_s, hs, b_w_o, j)
            new_k_p.append(kp.reshape(bp, tp, N_KV_HEADS, HEAD_DIM))
            new_v_p.append(vp.reshape(bp, tp, N_KV_HEADS, HEAD_DIM))
            new_k_s.append(ks_new.reshape(bs, ts, N_KV_HEADS, HEAD_DIM))
            new_v_s.append(vs_new.reshape(bs, ts, N_KV_HEADS, HEAD_DIM))
        n2 = norm2[i][None, :]
        hp = _moe(hp, n2, wr_all[i], br_all[i], wg_bf, wu_bf, wd_bf, i,
                  tm=min(1024, bp * tp), precise=False)
        hs = _moe(hs, n2, wr_all[i], br_all[i], moe_w_gate, moe_w_up, moe_w_down, i,
                  tm=bs * ts, precise=True)
    return (hp.reshape(bp, tp, d), hs.reshape(bs, ts, d),
            jnp.stack(new_k_p), jnp.stack(new_v_p), jnp.stack(new_k_s), jnp.stack(new_v_s),
            jnp.stack(new_chunk_v))
```

```python
import functools

import jax
import jax.numpy as jnp
from jax import lax
from jax.experimental import pallas as pl
from jax.experimental.pallas import tpu as pltpu

F32 = jnp.float32
BF16 = jnp.bfloat16

CHUNK = 128
CHUNK_GROUPS = 4
HEAD_DIM = 128
N_HEADS = 8
N_KV_HEADS = 4
KV_GROUP = N_HEADS // N_KV_HEADS
ROT_DIM = HEAD_DIM // 4
ROPE_THETA = 500000.0
MOBA_BLOCK = 256
MOBA_TOP_K = 3
N_EXPERT_GROUPS = 4
EXPERTS_PER_GROUP = 4
N_EXPERTS = N_EXPERT_GROUPS * EXPERTS_PER_GROUP
NORM_EPS = 1e-6
NEG_INF = -1e30
LOG2E = 1.4426950408889634
SPAN_BLOCKS = 2

SUBLANES = 8
LANES = 128
VMEM_LIMIT_BYTES = 56 * 1024 * 1024

_NT = (((1,), (1,)), ((), ()))


def _dot(a, b):
    return jnp.dot(a, b, preferred_element_type=F32)


def _dot_nt(a, b):
    return lax.dot_general(a, b, _NT, preferred_element_type=F32)


def _split_bf16(a):
    hi = a.astype(BF16)
    lo = (a - hi.astype(F32)).astype(BF16)
    return hi, lo


def _dot3(a, b, nt=False):
    d = _dot_nt if nt else _dot
    ah, al = _split_bf16(a)
    bh, bl = _split_bf16(b)
    return d(ah, bh) + (d(ah, bl) + d(al, bh))


def _mm(x, w, precise):
    if precise:
        return _dot3(x, w)
    return _dot(x.astype(BF16), w.astype(BF16))


def _rms(x, g):
    ms = jnp.mean(x * x, axis=-1, keepdims=True)
    return x * lax.rsqrt(ms + NORM_EPS) * g


def _gelu_tanh(x):
    c = 0.7978845608028654
    return 0.5 * x * (1.0 + jnp.tanh(c * (x + 0.044715 * (x * x * x))))


def _silu(x):
    return x * (1.0 / (1.0 + jnp.exp(-x)))


def _cparams(sem):
    return pltpu.CompilerParams(dimension_semantics=sem, vmem_limit_bytes=VMEM_LIMIT_BYTES)


def _layer_spec(shape, layer):
    zeros = (0,) * len(shape)
    return pl.BlockSpec((None,) + tuple(shape), lambda *_: (layer,) + zeros)


def _mixer_a_kernel(h_ref, n1_ref, win_ref, gv_ref, wmix_ref, bs_ref, wout_ref, *rest,
                    seq_len, emit_v, precise, nsplit):
    if emit_v:
        out_ref, v_ref, us_scr = rest
    else:
        out_ref, us_scr = rest
    width = gv_ref.shape[1]
    gdim = width // CHUNK_GROUPS
    sub = h_ref.shape[0] // nsplit
    row = lax.broadcasted_iota(jnp.int32, (CHUNK, CHUNK), 0)
    col = lax.broadcasted_iota(jnp.int32, (CHUNK, CHUNK), 1)
    mask = col <= row
    if seq_len < CHUNK:
        mask = mask & ((row // seq_len) == (col // seq_len))
    wms = [jnp.where(mask, wmix_ref[g], 0.0) for g in range(CHUNK_GROUPS)]
    for sl in range(nsplit):
        srows = slice(sl * sub, (sl + 1) * sub)
        x = h_ref[srows, :]
        xn = _rms(x, n1_ref[...])
        z = _gelu_tanh(_mm(xn, win_ref[...], precise))
        u = z[:, :width]
        vn = _rms(z[:, width:], gv_ref[...])
        if emit_v:
            v_ref[srows, :] = vn
        if not precise:
            vn = vn.astype(BF16)
        for g in range(CHUNK_GROUPS):
            bias = bs_ref[g]
            for c in range(sub // CHUNK):
                rows = slice(c * CHUNK, (c + 1) * CHUNK)
                cols = slice(g * gdim, (g + 1) * gdim)
                s = _mm(wms[g], vn[rows, cols], precise) + bias
                us_scr[sl * sub + c * CHUNK:sl * sub + (c + 1) * CHUNK, cols] = (
                    (u[rows, cols] * s).astype(us_scr.dtype))
        out_ref[srows, :] = x + _mm(us_scr[srows, :], wout_ref[...], precise)


def _mixer_a(h, n1, w_in, g_v, wmix, bs, w_out, layer, *, tm, seq_len, emit_v, precise, nsplit=1):
    n, d = h.shape
    width = g_v.shape[1]
    const = lambda i: (0, 0)
    out_shape = [jax.ShapeDtypeStruct((n, d), F32)]
    out_specs = [pl.BlockSpec((tm, d), lambda i: (i, 0))]
    if emit_v:
        out_shape.append(jax.ShapeDtypeStruct((n, width), F32))
        out_specs.append(pl.BlockSpec((tm, width), lambda i: (i, 0)))
    res = pl.pallas_call(
        functools.partial(_mixer_a_kernel, seq_len=seq_len, emit_v=emit_v, precise=precise,
                          nsplit=nsplit),
        grid=(n // tm,),
        in_specs=[
            pl.BlockSpec((tm, d), lambda i: (i, 0)),
            pl.BlockSpec((1, d), const),
            _layer_spec((d, 2 * width), layer),
            pl.BlockSpec((1, width), const),
            pl.BlockSpec((CHUNK_GROUPS, CHUNK, CHUNK), lambda i: (0, 0, 0)),
            pl.BlockSpec((CHUNK_GROUPS, CHUNK, 1), lambda i: (0, 0, 0)),
            _layer_spec((width, d), layer),
        ],
        out_specs=out_specs,
        out_shape=out_shape,
        scratch_shapes=[pltpu.VMEM((tm, width), F32 if precise else BF16)],
        compiler_params=_cparams(("parallel",)),
        name="mixer_a_sample" if emit_v else "mixer_a",
    )(h, n1, w_in, g_v, wmix, bs, w_out)
    return res if emit_v else res[0]


def _qkv_kernel(h_ref, n1_ref, w_ref, qn_ref, kn_ref, cos_ref, sa_ref, sb_ref,
                q_ref, k_ref, v_ref, *, precise, nsplit):
    nq = q_ref.shape[1]
    nk = N_KV_HEADS * HEAD_DIM
    sub = h_ref.shape[0] // nsplit
    for s in range(nsplit):
        rows = slice(s * sub, (s + 1) * sub)
        xn = _rms(h_ref[rows, :], n1_ref[...])
        qkv = _mm(xn, w_ref[...], precise)
        cos = cos_ref[rows, :]
        sa = sa_ref[rows, :]
        sb = sb_ref[rows, :]

        def head(t, g, cos=cos, sa=sa, sb=sb):
            t = _rms(t, g)
            return (t * cos + pltpu.roll(t, HEAD_DIM - ROT_DIM // 2, 1) * sa
                    + pltpu.roll(t, ROT_DIM // 2, 1) * sb)

        for hd in range(nq // HEAD_DIM):
            cols = slice(hd * HEAD_DIM, (hd + 1) * HEAD_DIM)
            q_ref[rows, cols] = head(qkv[:, cols], qn_ref[...])
        for hd in range(N_KV_HEADS):
            kv_rows = pl.ds(s * sub * N_KV_HEADS + hd, sub, stride=N_KV_HEADS)
            k_ref[kv_rows, :] = head(qkv[:, nq + hd * HEAD_DIM: nq + (hd + 1) * HEAD_DIM],
                                     kn_ref[...])
            v_ref[kv_rows, :] = qkv[:, nq + nk + hd * HEAD_DIM: nq + nk + (hd + 1) * HEAD_DIM]


def _rope_tables(pos):
    half = ROT_DIM // 2
    inv_freq = jnp.power(ROPE_THETA, -jnp.arange(half, dtype=F32) * 2.0 / ROT_DIM)
    ang = pos.astype(F32)[:, None] * inv_freq[None, :]
    cos = jnp.cos(ang)
    sin = jnp.sin(ang)
    t = pos.shape[0]
    ones = jnp.ones((t, HEAD_DIM - ROT_DIM), F32)
    zeros = jnp.zeros((t, HEAD_DIM - ROT_DIM), F32)
    z16 = jnp.zeros((t, half), F32)
    cos_t = jnp.concatenate([cos, cos, ones], axis=1)
    sa_t = jnp.concatenate([-sin, z16, zeros], axis=1)
    sb_t = jnp.concatenate([z16, sin, zeros], axis=1)
    return cos_t, sa_t, sb_t


def _qkv(h, n1, w, layer, qn, kn, tables, *, tm, precise, nsplit=1):
    n, d = h.shape
    nq = N_HEADS * HEAD_DIM
    nk = N_KV_HEADS * HEAD_DIM
    t = tables[0].shape[0]
    tiles_per_seq = t // tm
    const = lambda i: (0, 0)
    tab = pl.BlockSpec((tm, HEAD_DIM), lambda i: (i % tiles_per_seq, 0))
    return pl.pallas_call(
        functools.partial(_qkv_kernel, precise=precise, nsplit=nsplit),
        grid=(n // tm,),
        in_specs=[
            pl.BlockSpec((tm, d), lambda i: (i, 0)),
            pl.BlockSpec((1, d), const),
            _layer_spec((d, nq + 2 * nk), layer),
            pl.BlockSpec((1, HEAD_DIM), const),
            pl.BlockSpec((1, HEAD_DIM), const),
            tab, tab, tab,
        ],
        out_specs=[
            pl.BlockSpec((tm, nq), lambda i: (i, 0)),
            pl.BlockSpec((tm * N_KV_HEADS, HEAD_DIM), lambda i: (i, 0)),
            pl.BlockSpec((tm * N_KV_HEADS, HEAD_DIM), lambda i: (i, 0)),
        ],
        out_shape=[
            jax.ShapeDtypeStruct((n, nq), F32),
            jax.ShapeDtypeStruct((n * N_KV_HEADS, HEAD_DIM), F32),
            jax.ShapeDtypeStruct((n * N_KV_HEADS, HEAD_DIM), F32),
        ],
        compiler_params=_cparams(("parallel",)),
        name="qkv_sample" if precise else "qkv",
    )(h, n1, w, qn, kn, *tables)


def _attn_prompt_kernel(q_ref, k_ref, v_ref, h_ref, wo_ref, out_ref,
                        kbf, vtb, kmean, s_scr, p_scr, o_scr):
    blk = MOBA_BLOCK
    nblk = kbf.shape[0] // blk
    own = pl.program_id(1)
    scale = HEAD_DIM ** -0.5

    @pl.when(own == 0)
    def _():
        for n in range(nblk):
            rows = slice(n * blk, (n + 1) * blk)
            for kh in range(N_KV_HEADS):
                kcols = slice(kh * HEAD_DIM, (kh + 1) * HEAD_DIM)
                src_rows = pl.ds(n * blk * N_KV_HEADS + kh, blk, stride=N_KV_HEADS)
                kk = k_ref[src_rows, :]
                kbf[rows, kcols] = kk.astype(BF16)
                km_hi, km_lo = _split_bf16(jnp.mean(kk, axis=0, keepdims=True))
                kmean[n:n + 1, kcols] = km_hi.astype(F32)
                kmean[nblk + n:nblk + n + 1, kcols] = km_lo.astype(F32)
                vtb[kh, :HEAD_DIM, rows] = v_ref[src_rows, :].T.astype(BF16)
        for kh in range(N_KV_HEADS):
            vtb[kh, HEAD_DIM:, :] = jnp.ones((vtb.shape[1] - HEAD_DIM, vtb.shape[2]), BF16)

    cols = KV_GROUP * blk
    nidx = lax.broadcasted_iota(jnp.int32, (nblk, cols), 0)
    kq_diff = (lax.broadcasted_iota(jnp.int32, (blk, cols), 0)
               - lax.broadcasted_iota(jnp.int32, (blk, cols), 1) % blk)

    for kh in range(N_KV_HEADS):
        kcols = slice(kh * HEAD_DIM, (kh + 1) * HEAD_DIM)
        q2 = jnp.concatenate(
            [q_ref[:, (KV_GROUP * kh + i) * HEAD_DIM:(KV_GROUP * kh + i + 1) * HEAD_DIM]
             for i in range(KV_GROUP)], axis=0)
        q2s = (q2 * (scale * LOG2E)).astype(BF16)

        g2 = _dot_nt(kmean[:, kcols].astype(BF16), q2s)
        g_t = g2[:nblk] + g2[nblk:]
        rank = jnp.zeros_like(g_t)
        for m in range(nblk - 1):
            gm = g_t[m:m + 1, :]
            beats = (gm > g_t) | ((gm == g_t) & (nidx > m))
            inc = jnp.where(m < own, 1.0, 0.0)
            rank = rank + jnp.where(beats, inc, 0.0)
        sel_t = jnp.where(nidx == own, 1.0,
                          jnp.where((nidx < own) & (rank < float(MOBA_TOP_K)), 1.0, 0.0))

        def attend(nb, kh=kh, kcols=kcols, q2s=q2s, sel_t=sel_t):
            m = jnp.full((1, cols), NEG_INF, F32)
            for n in range(nb):
                rows = slice(n * blk, (n + 1) * blk)
                s = _dot_nt(kbf[rows, kcols], q2s)
                s = jnp.where(sel_t[n:n + 1, :] > 0.5, s, NEG_INF)
                if n >= nb - SPAN_BLOCKS:
                    s = jnp.where(kq_diff <= (own - n) * blk, s, NEG_INF)
                s_scr[rows, :] = s
                m = jnp.maximum(m, jnp.max(s, axis=0, keepdims=True))
            for n in range(nb):
                rows = slice(n * blk, (n + 1) * blk)
                p_scr[rows, :] = jnp.exp2(s_scr[rows, :] - m).astype(BF16)
            acc = _dot(vtb[kh, :, :nb * blk], p_scr[:nb * blk, :])
            o = (acc[:HEAD_DIM] / acc[HEAD_DIM:HEAD_DIM + 1]).T.astype(BF16)
            for i in range(KV_GROUP):
                hd = KV_GROUP * kh + i
                o_scr[:, hd * HEAD_DIM:(hd + 1) * HEAD_DIM] = o[i * blk:(i + 1) * blk]

        for nb in range(SPAN_BLOCKS, nblk + SPAN_BLOCKS, SPAN_BLOCKS):
            pl.when((own >= nb - SPAN_BLOCKS) & (own < nb))(functools.partial(attend, min(nb, nblk)))

    out_ref[...] = h_ref[...] + _dot(o_scr[...], wo_ref[...])


def _attn_prompt(q, k, v, h, wo, layer, *, batch, seq):
    blk = MOBA_BLOCK
    nqb = seq // blk
    d = h.shape[1]
    nq = q.shape[1]
    nk = N_KV_HEADS * HEAD_DIM
    cols = KV_GROUP * blk
    return pl.pallas_call(
        _attn_prompt_kernel,
        grid=(batch, nqb),
        in_specs=[
            pl.BlockSpec((blk, nq), lambda b, i: (b * nqb + i, 0)),
            pl.BlockSpec((seq * N_KV_HEADS, HEAD_DIM), lambda b, i: (b, 0)),
            pl.BlockSpec((seq * N_KV_HEADS, HEAD_DIM), lambda b, i: (b, 0)),
            pl.BlockSpec((blk, d), lambda b, i: (b * nqb + i, 0)),
            _layer_spec((nq, d), layer),
        ],
        out_specs=pl.BlockSpec((blk, d), lambda b, i: (b * nqb + i, 0)),
        out_shape=jax.ShapeDtypeStruct(h.shape, F32),
        scratch_shapes=[
            pltpu.VMEM((seq, nk), BF16),
            pltpu.VMEM((N_KV_HEADS, HEAD_DIM + 2 * SUBLANES, seq), BF16),
            pltpu.VMEM((2 * nqb, nk), F32),
            pltpu.VMEM((seq, cols), F32),
            pltpu.VMEM((seq, cols), BF16),
            pltpu.VMEM((blk, nq), BF16),
        ],
        compiler_params=_cparams(("parallel", "arbitrary")),
        name="attn_prompt",
    )(q, k, v, h, wo)


def _attn_sample_kernel(pt_ref, qd_ref, knew_ref, vnew_ref, ck_hbm, cv_hbm, out_ref,
                        kbuf, vbuf, s_scr, p_scr, kmean_scr, ksem, vsem,
                        *, base, npg, dec_seq):
    b = pl.program_id(0)
    nbatch = pl.num_programs(0)
    n_pages, prow, _ = kbuf.shape
    nchunk = n_pages // npg
    ccols = npg * prow
    bcols = MOBA_BLOCK * N_KV_HEADS
    pages_per_blk = bcols // prow
    blk_per_chunk = npg // pages_per_blk
    nrow = qd_ref.shape[0]
    nl = kmean_scr.shape[0]
    nblk = nl // SUBLANES
    scale = HEAD_DIM ** -0.5

    def page_copy(hbm, buf, sem, bb, pg):
        return pltpu.make_async_copy(hbm.at[base + pt_ref[bb, pg]], buf.at[pg], sem.at[pg // npg])

    def fetch(hbm, buf, sem, bb):
        for pg in range(n_pages):
            page_copy(hbm, buf, sem, bb, pg).start()

    def wait_chunk(hbm, buf, sem, c):
        for pg in range(c * npg, (c + 1) * npg):
            page_copy(hbm, buf, sem, b, pg).wait()

    @pl.when(b == 0)
    def _():
        fetch(ck_hbm, kbuf, ksem, 0)
        fetch(cv_hbm, vbuf, vsem, 0)

    qd = qd_ref[...]
    q_hi, q_lo = _split_bf16(qd)
    q_hl = jnp.concatenate([q_hi, q_lo], axis=0)

    for c in range(nchunk):
        wait_chunk(ck_hbm, kbuf, ksem, c)
        kf = [kbuf[c * npg + i] for i in range(npg)]
        for i in range(blk_per_chunk):
            tot = kf[pages_per_blk * i]
            for j in range(1, pages_per_blk):
                tot = tot + kf[pages_per_blk * i + j]
            part = jnp.sum(tot.reshape(prow // SUBLANES, SUBLANES, HEAD_DIM), axis=0)
            part = (part + pltpu.roll(part, N_KV_HEADS, 0)) * (1.0 / MOBA_BLOCK)
            row0 = (c * blk_per_chunk + i) * SUBLANES
            kmean_scr[row0:row0 + SUBLANES, :] = part
        k_hi, k_lo = _split_bf16(jnp.concatenate(kf, axis=0))
        s2 = _dot_nt(q_hl, k_hi)
        s_scr[:, c * ccols:(c + 1) * ccols] = (s2[:nrow] + s2[nrow:]) + _dot_nt(q_hi, k_lo)

    @pl.when(b + 1 < nbatch)
    def _():
        fetch(ck_hbm, kbuf, ksem, b + 1)

    g = _dot3(qd, kmean_scr[...], nt=True)
    lane = lax.broadcasted_iota(jnp.int32, (nrow, nl), 1)
    kv_r = (lax.broadcasted_iota(jnp.int32, (nrow, nl), 0) // dec_seq) // KV_GROUP
    rowsel = [jnp.where(kv_r == kh, 1.0, 0.0) for kh in range(N_KV_HEADS)]
    rank = jnp.zeros_like(g)
    for m in range(nblk):
        for kh in range(N_KV_HEADS):
            pos = m * SUBLANES + kh
            col = g[:, pos:pos + 1]
            beats = (col > g) | ((col == g) & (lane > pos))
            rank = rank + jnp.where(beats, rowsel[kh], 0.0)
    sel = jnp.where((lane % SUBLANES) == kv_r,
                    jnp.where(rank < float(MOBA_TOP_K), 1.0, 0.0), 0.0)
    blkid = lane // SUBLANES

    kv_c = (lax.broadcasted_iota(jnp.int32, (nrow, 1), 0) // dec_seq) // KV_GROUP
    t_row = lax.broadcasted_iota(jnp.int32, (nrow, 1), 0) % dec_seq

    def rows_for(x, t):
        out = jnp.zeros((nrow, HEAD_DIM), F32)
        for kh in range(N_KV_HEADS):
            r = t * N_KV_HEADS + kh
            out = out + jnp.where(kv_c == kh, x[r:r + 1, :], 0.0)
        return out

    knew = knew_ref[...]
    vnew = vnew_ref[...]
    s_own = [jnp.sum(qd * rows_for(knew, t), axis=-1, keepdims=True) * scale
             for t in range(dec_seq)]
    ok_own = [t_row >= t for t in range(dec_seq)]

    kh_match = (lax.broadcasted_iota(jnp.int32, (nrow, bcols), 1) % N_KV_HEADS) == (
        (lax.broadcasted_iota(jnp.int32, (nrow, bcols), 0) // dec_seq) // KV_GROUP)
    m = jnp.full((nrow, 1), NEG_INF, F32)
    for t in range(dec_seq):
        m = jnp.maximum(m, jnp.where(ok_own[t], s_own[t], NEG_INF))
    for n in range(nblk):
        cs = slice(n * bcols, (n + 1) * bcols)
        pick = jnp.max(jnp.where(blkid == n, sel, 0.0), axis=1, keepdims=True)
        sc = jnp.where(pick > 0.5, s_scr[:, cs] * scale, NEG_INF)
        sc = jnp.where(kh_match, sc, NEG_INF)
        s_scr[:, cs] = sc
        m = jnp.maximum(m, jnp.max(sc, axis=-1, keepdims=True))
    l = jnp.zeros((nrow, 1), F32)
    for n in range(nblk):
        cs = slice(n * bcols, (n + 1) * bcols)
        p = jnp.exp(s_scr[:, cs] - m)
        l = l + jnp.sum(p, axis=-1, keepdims=True)
        p_hi, p_lo = _split_bf16(p)
        p_scr[:nrow, cs] = p_hi
        p_scr[nrow:, cs] = p_lo
    acc = jnp.zeros((nrow, HEAD_DIM), F32)
    for t in range(dec_seq):
        p_t = jnp.where(ok_own[t], jnp.exp(s_own[t] - m), 0.0)
        l = l + p_t
        acc = acc + p_t * rows_for(vnew, t)

    for c in range(nchunk):
        wait_chunk(cv_hbm, vbuf, vsem, c)
        vb = jnp.concatenate([vbuf[c * npg + i].astype(BF16) for i in range(npg)], axis=0)
        pv = _dot(p_scr[:, c * ccols:(c + 1) * ccols], vb)
        acc = acc + (pv[:nrow] + pv[nrow:])

    @pl.when(b + 1 < nbatch)
    def _():
        fetch(cv_hbm, vbuf, vsem, b + 1)

    out_ref[...] = acc / l


def _attn_sample(q, k_new, v_new, cache_k, cache_v, page_table, layer, *, n_pool, batch, dec_seq,
                 npg=8):
    n_pages = page_table.shape[1]
    prow = cache_k.shape[1]
    page = prow // N_KV_HEADS
    assert (n_pages * page) % MOBA_BLOCK == 0 and dec_seq <= MOBA_BLOCK
    assert n_pages % npg == 0 and MOBA_BLOCK % page == 0 and npg % (MOBA_BLOCK // page) == 0
    assert SUBLANES == 2 * N_KV_HEADS and prow % SUBLANES == 0
    nchunk = n_pages // npg
    nblk = n_pages * page // MOBA_BLOCK
    nrow = N_HEADS * dec_seq
    qd = q.reshape(batch, dec_seq, N_HEADS, HEAD_DIM).transpose(0, 2, 1, 3)
    qd = qd.reshape(batch, nrow, HEAD_DIM)
    per_b = lambda b, pt: (b, 0, 0)
    new_rows = dec_seq * N_KV_HEADS
    grid_spec = pltpu.PrefetchScalarGridSpec(
        num_scalar_prefetch=1,
        grid=(batch,),
        in_specs=[
            pl.BlockSpec((None, nrow, HEAD_DIM), per_b),
            pl.BlockSpec((None, new_rows, HEAD_DIM), per_b),
            pl.BlockSpec((None, new_rows, HEAD_DIM), per_b),
            pl.BlockSpec(memory_space=pl.ANY),
            pl.BlockSpec(memory_space=pl.ANY),
        ],
        out_specs=pl.BlockSpec((None, nrow, HEAD_DIM), per_b),
        scratch_shapes=[
            pltpu.VMEM((n_pages, prow, HEAD_DIM), F32),
            pltpu.VMEM((n_pages, prow, HEAD_DIM), F32),
            pltpu.VMEM((nrow, n_pages * prow), F32),
            pltpu.VMEM((2 * nrow, n_pages * prow), BF16),
            pltpu.VMEM((nblk * SUBLANES, HEAD_DIM), F32),
            pltpu.SemaphoreType.DMA((nchunk,)),
            pltpu.SemaphoreType.DMA((nchunk,)),
        ],
    )
    o = pl.pallas_call(
        functools.partial(_attn_sample_kernel, base=layer * n_pool, npg=npg, dec_seq=dec_seq),
        grid_spec=grid_spec,
        out_shape=jax.ShapeDtypeStruct((batch, nrow, HEAD_DIM), F32),
        compiler_params=_cparams(("arbitrary",)),
        name="attn_sample",
    )(page_table, qd, k_new.reshape(batch, new_rows, HEAD_DIM),
      v_new.reshape(batch, new_rows, HEAD_DIM), cache_k, cache_v)
    o = o.reshape(batch, N_HEADS, dec_seq, HEAD_DIM).transpose(0, 2, 1, 3)
    return o.reshape(batch * dec_seq, N_HEADS * HEAD_DIM)


def _oproj_kernel(o_ref, h_ref, wo_ref, out_ref):
    out_ref[...] = h_ref[...] + _dot3(o_ref[...], wo_ref[...])


def _oproj(o, h, wo, layer):
    n, d = h.shape
    full = pl.BlockSpec((n, d), lambda i: (0, 0))
    return pl.pallas_call(
        _oproj_kernel,
        grid=(1,),
        in_specs=[pl.BlockSpec(o.shape, lambda i: (0, 0)), full,
                  _layer_spec(wo.shape[1:], layer)],
        out_specs=full,
        out_shape=jax.ShapeDtypeStruct(h.shape, F32),
        compiler_params=_cparams(("arbitrary",)),
        name="oproj_sample",
    )(o, h, wo)


def _route(logits, lane):
    first_expert_lane = N_EXPERT_GROUPS
    is_group = lane < N_EXPERT_GROUPS
    gl = jnp.where(is_group, logits, NEG_INF)
    ge = jnp.where(is_group, jnp.exp(gl - jnp.max(gl, axis=-1, keepdims=True)), 0.0)
    gprob = ge / jnp.sum(ge, axis=-1, keepdims=True)
    g_w = jnp.max(gprob, axis=-1, keepdims=True)
    g_idx = jnp.min(jnp.where(is_group & (gprob == g_w), lane, LANES), axis=-1, keepdims=True)
    lo = first_expert_lane + g_idx * EXPERTS_PER_GROUP
    in_group = (lane >= lo) & (lane < lo + EXPERTS_PER_GROUP)
    el = jnp.where(in_group, logits, NEG_INF)
    ee = jnp.where(in_group, jnp.exp(el - jnp.max(el, axis=-1, keepdims=True)), 0.0)
    eprob = jnp.where(in_group, ee / jnp.sum(ee, axis=-1, keepdims=True), -1.0)
    p1 = jnp.max(eprob, axis=-1, keepdims=True)
    i1 = jnp.min(jnp.where(eprob == p1, lane, LANES), axis=-1, keepdims=True)
    rest = jnp.where(lane == i1, -1.0, eprob)
    p2 = jnp.max(rest, axis=-1, keepdims=True)
    i2 = jnp.min(jnp.where(rest == p2, lane, LANES), axis=-1, keepdims=True)
    denom = p1 + p2
    w1 = p1 / denom * g_w
    w2 = p2 / denom * g_w
    return g_idx, jnp.where(lane == i1, w1, 0.0) + jnp.where(lane == i2, w2, 0.0)


def _moe_kernel(h_ref, n2_ref, wr_ref, br_ref, wg_ref, wu_ref, wd_ref, out_ref,
                xn_scr, gates_scr, hid_scr, *, precise):
    g = pl.program_id(1)
    tm = h_ref.shape[0]
    de = wg_ref.shape[2]
    lane = lax.broadcasted_iota(jnp.int32, (tm, LANES), 1)
    first_expert_lane = N_EXPERT_GROUPS

    @pl.when(g == 0)
    def _():
        x = h_ref[...]
        xn = _rms(x, n2_ref[...])
        xn_scr[...] = xn.astype(xn_scr.dtype)
        out_ref[...] = x
        _, gates_scr[...] = _route(_mm(xn, wr_ref[...], precise) + br_ref[...], lane)

    xb = xn_scr[...]
    gates = gates_scr[...]
    for e in range(EXPERTS_PER_GROUP):
        a = _mm(xb, wg_ref[e], precise)
        b = _mm(xb, wu_ref[e], precise)
        tgt = first_expert_lane + g * EXPERTS_PER_GROUP + e
        gate = jnp.sum(jnp.where(lane == tgt, gates, 0.0), axis=-1, keepdims=True)
        hid_scr[:, e * de:(e + 1) * de] = (_silu(a) * b * gate).astype(hid_scr.dtype)
    out_ref[...] += _mm(hid_scr[...], wd_ref[...], precise)


def _moe(h, n2, wr, br, wg, wu, wd, layer, *, tm, precise):
    n, d = h.shape
    de = wg.shape[3]
    epg = EXPERTS_PER_GROUP
    const = lambda i, g: (0, 0)
    group = lambda i, g: (layer, g, 0, 0)
    wd = wd.reshape(wd.shape[0], N_EXPERT_GROUPS, epg * de, d)
    work = F32 if precise else BF16
    return pl.pallas_call(
        functools.partial(_moe_kernel, precise=precise),
        grid=(n // tm, N_EXPERT_GROUPS),
        in_specs=[
            pl.BlockSpec((tm, d), lambda i, g: (i, 0)),
            pl.BlockSpec((1, d), const),
            pl.BlockSpec((d, LANES), const),
            pl.BlockSpec((1, LANES), const),
            pl.BlockSpec((None, epg, d, de), group),
            pl.BlockSpec((None, epg, d, de), group),
            pl.BlockSpec((None, None, epg * de, d), group),
        ],
        out_specs=pl.BlockSpec((tm, d), lambda i, g: (i, 0)),
        out_shape=jax.ShapeDtypeStruct((n, d), F32),
        scratch_shapes=[
            pltpu.VMEM((tm, d), work),
            pltpu.VMEM((tm, LANES), F32),
            pltpu.VMEM((tm, epg * de), work),
        ],
        compiler_params=_cparams(("parallel", "arbitrary")),
        name="moe_sample" if precise else "moe",
    )(h, n2, wr, br, wg, wu, wd)


def kernel(x_prompt, x_sample, cache_k, cache_v, page_table, norm1, norm2, a_w_in, a_g_v, a_w_s, a_b_s, a_w_out, b_w_qkv, b_q_norm, b_k_norm, b_w_o, moe_w_group, moe_b_group, moe_w_router, moe_b_router, moe_w_gate, moe_w_up, moe_w_down):
    bp, tp, d = x_prompt.shape
    bs, ts, _ = x_sample.shape
    depth = norm1.shape[0]
    n_attn, n_pool, page = cache_k.shape[0], cache_k.shape[1], cache_k.shape[2]
    past_len = page_table.shape[1] * page
    assert CHUNK % ts == 0 and (bs * ts) % CHUNK == 0 and tp % MOBA_BLOCK == 0

    hp = x_prompt.reshape(bp * tp, d)
    hs = x_sample.reshape(bs * ts, d)
    ck = cache_k.reshape(n_attn * n_pool, page * N_KV_HEADS, HEAD_DIM)
    cv = cache_v.reshape(n_attn * n_pool, page * N_KV_HEADS, HEAD_DIM)
    tab_p = _rope_tables(jnp.arange(tp, dtype=jnp.int32))
    pos_s = past_len + jnp.arange(ts, dtype=jnp.int32)
    tab_s = tuple(jnp.tile(t, (bs, 1)) for t in _rope_tables(pos_s))

    pad = LANES - N_EXPERT_GROUPS - N_EXPERTS
    wr_all = jnp.concatenate(
        [moe_w_group, moe_w_router, jnp.zeros((depth, d, pad), F32)], axis=2)
    br_all = jnp.concatenate(
        [moe_b_group, moe_b_router, jnp.zeros((depth, pad), F32)], axis=1)[:, None, :]

    w_in_bf, w_out_bf = a_w_in.astype(BF16), a_w_out.astype(BF16)
    w_qkv_bf, w_o_bf = b_w_qkv.astype(BF16), b_w_o.astype(BF16)
    wg_bf, wu_bf, wd_bf = moe_w_gate.astype(BF16), moe_w_up.astype(BF16), moe_w_down.astype(BF16)

    new_k_p, new_v_p, new_k_s, new_v_s, new_chunk_v = [], [], [], [], []
    reps = CHUNK // ts
    for i in range(depth):
        j = i // 2
        n1 = norm1[i][None, :]
        if i % 2 == 0:
            g_v = a_g_v[j][None, :]
            hp = _mixer_a(hp, n1, w_in_bf, g_v, a_w_s[j], a_b_s[j][:, :, None], w_out_bf, j,
                          tm=1024, seq_len=CHUNK, emit_v=False, precise=False, nsplit=2)
            wmix_s = jnp.tile(a_w_s[j][:, :ts, :ts], (1, reps, reps))
            bs_s = jnp.tile(a_b_s[j][:, :ts], (1, reps))[:, :, None]
            hs, v_rows = _mixer_a(hs, n1, a_w_in, g_v, wmix_s, bs_s, a_w_out, j,
                                  tm=CHUNK, seq_len=ts, emit_v=True, precise=True)
            new_chunk_v.append(v_rows.reshape(bs, ts, -1))
        else:
            qn = b_q_norm[j][None, :]
            kn = b_k_norm[j][None, :]
            qp, kp, vp = _qkv(hp, n1, w_qkv_bf, j, qn, kn, tab_p, tm=1024, precise=False, nsplit=2)
            qs, ks_new, vs_new = _qkv(hs, n1, b_w_qkv, j, qn, kn, tab_s, tm=bs * ts, precise=True)
            hp = _attn_prompt(qp, kp, vp, hp, w_o_bf, j, batch=bp, seq=tp)
            o_s = _attn_sample(qs, ks_new, vs_new, ck, cv, page_table, j, n_pool=n_pool,
                               batch=bs, dec_seq=ts)
            hs = _oproj(o_s, hs, b_w_o, j)
            new_k_p.append(kp.reshape(bp, tp, N_KV_HEADS, HEAD_DIM))
            new_v_p.append(vp.reshape(bp, tp, N_KV_HEADS, HEAD_DIM))
            new_k_s.append(ks_new.reshape(bs, ts, N_KV_HEADS, HEAD_DIM))
            new_v_s.append(vs_new.reshape(bs, ts, N_KV_HEADS, HEAD_DIM))
        n2 = norm2[i][None, :]
        hp = _moe(hp, n2, wr_all[i], br_all[i], wg_bf, wu_bf, wd_bf, i,
                  tm=min(1024, bp * tp), precise=False)
        hs = _moe(hs, n2, wr_all[i], br_all[i], moe_w_gate, moe_w_up, moe_w_down, i,
                  tm=bs * ts, precise=True)
    return (hp.reshape(bp, tp, d), hs.reshape(bs, ts, d),
            jnp.stack(new_k_p), jnp.stack(new_v_p), jnp.stack(new_k_s), jnp.stack(new_v_s),
            jnp.stack(new_chunk_v))
```

```python
import functools

import jax
import jax.numpy as jnp
from jax import lax
from jax.experimental import pallas as pl
from jax.experimental.pallas import tpu as pltpu

F32 = jnp.float32
BF16 = jnp.bfloat16

CHUNK = 128
CHUNK_GROUPS = 4
HEAD_DIM = 128
N_HEADS = 8
N_KV_HEADS = 4
KV_GROUP = N_HEADS // N_KV_HEADS
ROT_DIM = HEAD_DIM // 4
ROPE_THETA = 500000.0
MOBA_BLOCK = 256
MOBA_TOP_K = 3
N_EXPERT_GROUPS = 4
EXPERTS_PER_GROUP = 4
N_EXPERTS = N_EXPERT_GROUPS * EXPERTS_PER_GROUP
NORM_EPS = 1e-6
NEG_INF = -1e30
LOG2E = 1.4426950408889634
SPAN_BLOCKS = 2

SUBLANES = 8
LANES = 128
VMEM_LIMIT_BYTES = 56 * 1024 * 1024

_NT = (((1,), (1,)), ((), ()))


def _dot(a, b):
    return jnp.dot(a, b, preferred_element_type=F32)


def _dot_nt(a, b):
    return lax.dot_general(a, b, _NT, preferred_element_type=F32)


def _split_bf16(a):
    hi = a.astype(BF16)
    lo = (a - hi.astype(F32)).astype(BF16)
    return hi, lo


def _dot3(a, b, nt=False):
    d = _dot_nt if nt else _dot
    ah, al = _split_bf16(a)
    bh, bl = _split_bf16(b)
    return d(ah, bh) + (d(ah, bl) + d(al, bh))


def _mm(x, w, precise):
    if precise:
        return _dot3(x, w)
    return _dot(x.astype(BF16), w.astype(BF16))


def _rms(x, g):
    ms = jnp.mean(x * x, axis=-1, keepdims=True)
    return x * lax.rsqrt(ms + NORM_EPS) * g


def _gelu_tanh(x):
    c = 0.7978845608028654
    return 0.5 * x * (1.0 + jnp.tanh(c * (x + 0.044715 * (x * x * x))))


def _silu(x):
    return x * (1.0 / (1.0 + jnp.exp(-x)))


def _cparams(sem):
    return pltpu.CompilerParams(dimension_semantics=sem, vmem_limit_bytes=VMEM_LIMIT_BYTES)


def _layer_spec(shape, layer):
    zeros = (0,) * len(shape)
    return pl.BlockSpec((None,) + tuple(shape), lambda *_: (layer,) + zeros)


def _mixer_a_kernel(h_ref, n1_ref, win_ref, gv_ref, wmix_ref, bs_ref, wout_ref, *rest,
                    seq_len, emit_v, precise, nsplit):
    if emit_v:
        out_ref, v_ref, us_scr = rest
    else:
        out_ref, us_scr = rest
    width = gv_ref.shape[1]
    gdim = width // CHUNK_GROUPS
    sub = h_ref.shape[0] // nsplit
    row = lax.broadcasted_iota(jnp.int32, (CHUNK, CHUNK), 0)
    col = lax.broadcasted_iota(jnp.int32, (CHUNK, CHUNK), 1)
    mask = col <= row
    if seq_len < CHUNK:
        mask = mask & ((row // seq_len) == (col // seq_len))
    wms = [jnp.where(mask, wmix_ref[g], 0.0) for g in range(CHUNK_GROUPS)]
    for sl in range(nsplit):
        srows = slice(sl * sub, (sl + 1) * sub)
        x = h_ref[srows, :]
        xn = _rms(x, n1_ref[...])
        z = _gelu_tanh(_mm(xn, win_ref[...], precise))
        u = z[:, :width]
        vn = _rms(z[:, width:], gv_ref[...])
        if emit_v:
            v_ref[srows, :] = vn
        if not precise:
            vn = vn.astype(BF16)
        for g in range(CHUNK_GROUPS):
            bias = bs_ref[g]
            for c in range(sub // CHUNK):
                rows = slice(c * CHUNK, (c + 1) * CHUNK)
                cols = slice(g * gdim, (g + 1) * gdim)
                s = _mm(wms[g], vn[rows, cols], precise) + bias
                us_scr[sl * sub + c * CHUNK:sl * sub + (c + 1) * CHUNK, cols] = (
                    (u[rows, cols] * s).astype(us_scr.dtype))
        out_ref[srows, :] = x + _mm(us_scr[srows, :], wout_ref[...], precise)


def _mixer_a(h, n1, w_in, g_v, wmix, bs, w_out, layer, *, tm, seq_len, emit_v, precise, nsplit=1):
    n, d = h.shape
    width = g_v.shape[1]
    const = lambda i: (0, 0)
    out_shape = [jax.ShapeDtypeStruct((n, d), F32)]
    out_specs = [pl.BlockSpec((tm, d), lambda i: (i, 0))]
    if emit_v:
        out_shape.append(jax.ShapeDtypeStruct((n, width), F32))
        out_specs.append(pl.BlockSpec((tm, width), lambda i: (i, 0)))
    res = pl.pallas_call(
        functools.partial(_mixer_a_kernel, seq_len=seq_len, emit_v=emit_v, precise=precise,
                          nsplit=nsplit),
        grid=(n // tm,),
        in_specs=[
            pl.BlockSpec((tm, d), lambda i: (i, 0)),
            pl.BlockSpec((1, d), const),
            _layer_spec((d, 2 * width), layer),
            pl.BlockSpec((1, width), const),
            pl.BlockSpec((CHUNK_GROUPS, CHUNK, CHUNK), lambda i: (0, 0, 0)),
            pl.BlockSpec((CHUNK_GROUPS, CHUNK, 1), lambda i: (0, 0, 0)),
            _layer_spec((width, d), layer),
        ],
        out_specs=out_specs,
        out_shape=out_shape,
        scratch_shapes=[pltpu.VMEM((tm, width), F32 if precise else BF16)],
        compiler_params=_cparams(("parallel",)),
        name="mixer_a_sample" if emit_v else "mixer_a",
    )(h, n1, w_in, g_v, wmix, bs, w_out)
    return res if emit_v else res[0]


def _qkv_kernel(h_ref, n1_ref, w_ref, qn_ref, kn_ref, cos_ref, sa_ref, sb_ref,
                q_ref, k_ref, v_ref, *, precise, nsplit):
    nq = q_ref.shape[1]
    nk = N_KV_HEADS * HEAD_DIM
    sub = h_ref.shape[0] // nsplit
    for s in range(nsplit):
        rows = slice(s * sub, (s + 1) * sub)
        xn = _rms(h_ref[rows, :], n1_ref[...])
        qkv = _mm(xn, w_ref[...], precise)
        cos = cos_ref[rows, :]
        sa = sa_ref[rows, :]
        sb = sb_ref[rows, :]

        def head(t, g, cos=cos, sa=sa, sb=sb):
            t = _rms(t, g)
            return (t * cos + pltpu.roll(t, HEAD_DIM - ROT_DIM // 2, 1) * sa
                    + pltpu.roll(t, ROT_DIM // 2, 1) * sb)

        for hd in range(nq // HEAD_DIM):
            cols = slice(hd * HEAD_DIM, (hd + 1) * HEAD_DIM)
            q_ref[rows, cols] = head(qkv[:, cols], qn_ref[...])
        for hd in range(N_KV_HEADS):
            kv_rows = pl.ds(s * sub * N_KV_HEADS + hd, sub, stride=N_KV_HEADS)
            k_ref[kv_rows, :] = head(qkv[:, nq + hd * HEAD_DIM: nq + (hd + 1) * HEAD_DIM],
                                     kn_ref[...])
            v_ref[kv_rows, :] = qkv[:, nq + nk + hd * HEAD_DIM: nq + nk + (hd + 1) * HEAD_DIM]


def _rope_tables(pos):
    half = ROT_DIM // 2
    inv_freq = jnp.power(ROPE_THETA, -jnp.arange(half, dtype=F32) * 2.0 / ROT_DIM)
    ang = pos.astype(F32)[:, None] * inv_freq[None, :]
    cos = jnp.cos(ang)
    sin = jnp.sin(ang)
    t = pos.shape[0]
    ones = jnp.ones((t, HEAD_DIM - ROT_DIM), F32)
    zeros = jnp.zeros((t, HEAD_DIM - ROT_DIM), F32)
    z16 = jnp.zeros((t, half), F32)
    cos_t = jnp.concatenate([cos, cos, ones], axis=1)
    sa_t = jnp.concatenate([-sin, z16, zeros], axis=1)
    sb_t = jnp.concatenate([z16, sin, zeros], axis=1)
    return cos_t, sa_t, sb_t


def _qkv(h, n1, w, layer, qn, kn, tables, *, tm, precise, nsplit=1):
    n, d = h.shape
    nq = N_HEADS * HEAD_DIM
    nk = N_KV_HEADS * HEAD_DIM
    t = tables[0].shape[0]
    tiles_per_seq = t // tm
    const = lambda i: (0, 0)
    tab = pl.BlockSpec((tm, HEAD_DIM), lambda i: (i % tiles_per_seq, 0))
    return pl.pallas_call(
        functools.partial(_qkv_kernel, precise=precise, nsplit=nsplit),
        grid=(n // tm,),
        in_specs=[
            pl.BlockSpec((tm, d), lambda i: (i, 0)),
            pl.BlockSpec((1, d), const),
            _layer_spec((d, nq + 2 * nk), layer),
            pl.BlockSpec((1, HEAD_DIM), const),
            pl.BlockSpec((1, HEAD_DIM), const),
            tab, tab, tab,
        ],
        out_specs=[
            pl.BlockSpec((tm, nq), lambda i: (i, 0)),
            pl.BlockSpec((tm * N_KV_HEADS, HEAD_DIM), lambda i: (i, 0)),
            pl.BlockSpec((tm * N_KV_HEADS, HEAD_DIM), lambda i: (i, 0)),
        ],
        out_shape=[
            jax.ShapeDtypeStruct((n, nq), F32),
            jax.ShapeDtypeStruct((n * N_KV_HEADS, HEAD_DIM), F32),
            jax.ShapeDtypeStruct((n * N_KV_HEADS, HEAD_DIM), F32),
        ],
        compiler_params=_cparams(("parallel",)),
        name="qkv_sample" if precise else "qkv",
    )(h, n1, w, qn, kn, *tables)


def _attn_prompt_kernel(q_ref, k_ref, v_ref, h_ref, wo_ref, out_ref,
                        kbf, vtb, kmean, s_scr, p_scr, o_scr):
    blk = MOBA_BLOCK
    nblk = kbf.shape[0] // blk
    own = pl.program_id(1)
    scale = HEAD_DIM ** -0.5

    @pl.when(own == 0)
    def _():
        for n in range(nblk):
            rows = slice(n * blk, (n + 1) * blk)
            for kh in range(N_KV_HEADS):
                kcols = slice(kh * HEAD_DIM, (kh + 1) * HEAD_DIM)
                src_rows = pl.ds(n * blk * N_KV_HEADS + kh, blk, stride=N_KV_HEADS)
                kk = k_ref[src_rows, :]
                kbf[rows, kcols] = kk.astype(BF16)
                km_hi, km_lo = _split_bf16(jnp.mean(kk, axis=0, keepdims=True))
                kmean[n:n + 1, kcols] = km_hi.astype(F32)
                kmean[nblk + n:nblk + n + 1, kcols] = km_lo.astype(F32)
                vtb[kh, :HEAD_DIM, rows] = v_ref[src_rows, :].T.astype(BF16)
        for kh in range(N_KV_HEADS):
            vtb[kh, HEAD_DIM:, :] = jnp.ones((vtb.shape[1] - HEAD_DIM, vtb.shape[2]), BF16)

    cols = KV_GROUP * blk
    nidx = lax.broadcasted_iota(jnp.int32, (nblk, cols), 0)
    kq_diff = (lax.broadcasted_iota(jnp.int32, (blk, cols), 0)
               - lax.broadcasted_iota(jnp.int32, (blk, cols), 1) % blk)

    for kh in range(N_KV_HEADS):
        kcols = slice(kh * HEAD_DIM, (kh + 1) * HEAD_DIM)
        q2 = jnp.concatenate(
            [q_ref[:, (KV_GROUP * kh + i) * HEAD_DIM:(KV_GROUP * kh + i + 1) * HEAD_DIM]
             for i in range(KV_GROUP)], axis=0)
        q2s = (q2 * (scale * LOG2E)).astype(BF16)

        g2 = _dot_nt(kmean[:, kcols].astype(BF16), q2s)
        g_t = g2[:nblk] + g2[nblk:]
        rank = jnp.zeros_like(g_t)
        for m in range(nblk - 1):
            gm = g_t[m:m + 1, :]
            beats = (gm > g_t) | ((gm == g_t) & (nidx > m))
            inc = jnp.where(m < own, 1.0, 0.0)
            rank = rank + jnp.where(beats, inc, 0.0)
        sel_t = jnp.where(nidx == own, 1.0,
                          jnp.where((nidx < own) & (rank < float(MOBA_TOP_K)), 1.0, 0.0))

        def attend(nb, kh=kh, kcols=kcols, q2s=q2s, sel_t=sel_t):
            m = jnp.full((1, cols), NEG_INF, F32)
            for n in range(nb):
                rows = slice(n * blk, (n + 1) * blk)
                s = _dot_nt(kbf[rows, kcols], q2s)
                s = jnp.where(sel_t[n:n + 1, :] > 0.5, s, NEG_INF)
                if n >= nb - SPAN_BLOCKS:
                    s = jnp.where(kq_diff <= (own - n) * blk, s, NEG_INF)
                s_scr[rows, :] = s
                m = jnp.maximum(m, jnp.max(s, axis=0, keepdims=True))
            for n in range(nb):
                rows = slice(n * blk, (n + 1) * blk)
                p_scr[rows, :] = jnp.exp2(s_scr[rows, :] - m).astype(BF16)
            acc = _dot(vtb[kh, :, :nb * blk], p_scr[:nb * blk, :])
            o = (acc[:HEAD_DIM] / acc[HEAD_DIM:HEAD_DIM + 1]).T.astype(BF16)
            for i in range(KV_GROUP):
                hd = KV_GROUP * kh + i
                o_scr[:, hd * HEAD_DIM:(hd + 1) * HEAD_DIM] = o[i * blk:(i + 1) * blk]

        for nb in range(SPAN_BLOCKS, nblk + SPAN_BLOCKS, SPAN_BLOCKS):
            pl.when((own >= nb - SPAN_BLOCKS) & (own < nb))(functools.partial(attend, min(nb, nblk)))

    out_ref[...] = h_ref[...] + _dot(o_scr[...], wo_ref[...])


def _attn_prompt(q, k, v, h, wo, layer, *, batch, seq):
    blk = MOBA_BLOCK
    nqb = seq // blk
    d = h.shape[1]
    nq = q.shape[1]
    nk = N_KV_HEADS * HEAD_DIM
    cols = KV_GROUP * blk
    return pl.pallas_call(
        _attn_prompt_kernel,
        grid=(batch, nqb),
        in_specs=[
            pl.BlockSpec((blk, nq), lambda b, i: (b * nqb + i, 0)),
            pl.BlockSpec((seq * N_KV_HEADS, HEAD_DIM), lambda b, i: (b, 0)),
            pl.BlockSpec((seq * N_KV_HEADS, HEAD_DIM), lambda b, i: (b, 0)),
            pl.BlockSpec((blk, d), lambda b, i: (b * nqb + i, 0)),
            _layer_spec((nq, d), layer),
        ],
        out_specs=pl.BlockSpec((blk, d), lambda b, i: (b * nqb + i, 0)),
        out_shape=jax.ShapeDtypeStruct(h.shape, F32),
        scratch_shapes=[
            pltpu.VMEM((seq, nk), BF16),
            pltpu.VMEM((N_KV_HEADS, HEAD_DIM + 2 * SUBLANES, seq), BF16),
            pltpu.VMEM((2 * nqb, nk), F32),
            pltpu.VMEM((seq, cols), F32),
            pltpu.VMEM((seq, cols), BF16),
            pltpu.VMEM((blk, nq), BF16),
        ],
        compiler_params=_cparams(("parallel", "arbitrary")),
        name="attn_prompt",
    )(q, k, v, h, wo)


def _attn_sample_kernel(pt_ref, qd_ref, knew_ref, vnew_ref, ck_hbm, cv_hbm, out_ref,
                        kbuf, vbuf, s_scr, p_scr, kmean_scr, ksem, vsem,
                        *, base, npg, dec_seq):
    b = pl.program_id(0)
    nbatch = pl.num_programs(0)
    n_pages, prow, _ = kbuf.shape
    nchunk = n_pages // npg
    ccols = npg * prow
    bcols = MOBA_BLOCK * N_KV_HEADS
    pages_per_blk = bcols // prow
    blk_per_chunk = npg // pages_per_blk
    nrow = qd_ref.shape[0]
    nl = kmean_scr.shape[0]
    nblk = nl // SUBLANES
    scale = HEAD_DIM ** -0.5

    def page_copy(hbm, buf, sem, bb, pg):
        return pltpu.make_async_copy(hbm.at[base + pt_ref[bb, pg]], buf.at[pg], sem.at[pg // npg])

    def fetch(hbm, buf, sem, bb):
        for pg in range(n_pages):
            page_copy(hbm, buf, sem, bb, pg).start()

    def wait_chunk(hbm, buf, sem, c):
        for pg in range(c * npg, (c + 1) * npg):
            page_copy(hbm, buf, sem, b, pg).wait()

    @pl.when(b == 0)
    def _():
        fetch(ck_hbm, kbuf, ksem, 0)
        fetch(cv_hbm, vbuf, vsem, 0)

    qd = qd_ref[...]
    q_hi, q_lo = _split_bf16(qd)
    q_hl = jnp.concatenate([q_hi, q_lo], axis=0)

    for c in range(nchunk):
        wait_chunk(ck_hbm, kbuf, ksem, c)
        kf = [kbuf[c * npg + i] for i in range(npg)]
        for i in range(blk_per_chunk):
            tot = kf[pages_per_blk * i]
            for j in range(1, pages_per_blk):
                tot = tot + kf[pages_per_blk * i + j]
            part = jnp.sum(tot.reshape(prow // SUBLANES, SUBLANES, HEAD_DIM), axis=0)
            part = (part + pltpu.roll(part, N_KV_HEADS, 0)) * (1.0 / MOBA_BLOCK)
            row0 = (c * blk_per_chunk + i) * SUBLANES
            kmean_scr[row0:row0 + SUBLANES, :] = part
        k_hi, k_lo = _split_bf16(jnp.concatenate(kf, axis=0))
        s2 = _dot_nt(q_hl, k_hi)
        s_scr[:, c * ccols:(c + 1) * ccols] = (s2[:nrow] + s2[nrow:]) + _dot_nt(q_hi, k_lo)

    @pl.when(b + 1 < nbatch)
    def _():
        fetch(ck_hbm, kbuf, ksem, b + 1)

    g = _dot3(qd, kmean_scr[...], nt=True)
    lane = lax.broadcasted_iota(jnp.int32, (nrow, nl), 1)
    kv_r = (lax.broadcasted_iota(jnp.int32, (nrow, nl), 0) // dec_seq) // KV_GROUP
    rowsel = [jnp.where(kv_r == kh, 1.0, 0.0) for kh in range(N_KV_HEADS)]
    rank = jnp.zeros_like(g)
    for m in range(nblk):
        for kh in range(N_KV_HEADS):
            pos = m * SUBLANES + kh
            col = g[:, pos:pos + 1]
            beats = (col > g) | ((col == g) & (lane > pos))
            rank = rank + jnp.where(beats, rowsel[kh], 0.0)
    sel = jnp.where((lane % SUBLANES) == kv_r,
                    jnp.where(rank < float(MOBA_TOP_K), 1.0, 0.0), 0.0)
    blkid = lane // SUBLANES

    kv_c = (lax.broadcasted_iota(jnp.int32, (nrow, 1), 0) // dec_seq) // KV_GROUP
    t_row = lax.broadcasted_iota(jnp.int32, (nrow, 1), 0) % dec_seq

    def rows_for(x, t):
        out = jnp.zeros((nrow, HEAD_DIM), F32)
        for kh in range(N_KV_HEADS):
            r = t * N_KV_HEADS + kh
            out = out + jnp.where(kv_c == kh, x[r:r + 1, :], 0.0)
        return out

    knew = knew_ref[...]
    vnew = vnew_ref[...]
    s_own = [jnp.sum(qd * rows_for(knew, t), axis=-1, keepdims=True) * scale
             for t in range(dec_seq)]
    ok_own = [t_row >= t for t in range(dec_seq)]

    kh_match = (lax.broadcasted_iota(jnp.int32, (nrow, bcols), 1) % N_KV_HEADS) == (
        (lax.broadcasted_iota(jnp.int32, (nrow, bcols), 0) // dec_seq) // KV_GROUP)
    m = jnp.full((nrow, 1), NEG_INF, F32)
    for t in range(dec_seq):
        m = jnp.maximum(m, jnp.where(ok_own[t], s_own[t], NEG_INF))
    for n in range(nblk):
        cs = slice(n * bcols, (n + 1) * bcols)
        pick = jnp.max(jnp.where(blkid == n, sel, 0.0), axis=1, keepdims=True)
        sc = jnp.where(pick > 0.5, s_scr[:, cs] * scale, NEG_INF)
        sc = jnp.where(kh_match, sc, NEG_INF)
        s_scr[:, cs] = sc
        m = jnp.maximum(m, jnp.max(sc, axis=-1, keepdims=True))
    l = jnp.zeros((nrow, 1), F32)
    for n in range(nblk):
        cs = slice(n * bcols, (n + 1) * bcols)
        p = jnp.exp(s_scr[:, cs] - m)
        l = l + jnp.sum(p, axis=-1, keepdims=True)
        p_hi, p_lo = _split_bf16(p)
        p_scr[:nrow, cs] = p_hi
        p_scr[nrow:, cs] = p_lo
    acc = jnp.zeros((nrow, HEAD_DIM), F32)
    for t in range(dec_seq):
        p_t = jnp.where(ok_own[t], jnp.exp(s_own[t] - m), 0.0)
        l = l + p_t
        acc = acc + p_t * rows_for(vnew, t)

    for c in range(nchunk):
        wait_chunk(cv_hbm, vbuf, vsem, c)
        vb = jnp.concatenate([vbuf[c * npg + i].astype(BF16) for i in range(npg)], axis=0)
        pv = _dot(p_scr[:, c * ccols:(c + 1) * ccols], vb)
        acc = acc + (pv[:nrow] + pv[nrow:])

    @pl.when(b + 1 < nbatch)
    def _():
        fetch(cv_hbm, vbuf, vsem, b + 1)

    out_ref[...] = acc / l


def _attn_sample(q, k_new, v_new, cache_k, cache_v, page_table, layer, *, n_pool, batch, dec_seq,
                 npg=8):
    n_pages = page_table.shape[1]
    prow = cache_k.shape[1]
    page = prow // N_KV_HEADS
    assert (n_pages * page) % MOBA_BLOCK == 0 and dec_seq <= MOBA_BLOCK
    assert n_pages % npg == 0 and MOBA_BLOCK % page == 0 and npg % (MOBA_BLOCK // page) == 0
    assert SUBLANES == 2 * N_KV_HEADS and prow % SUBLANES == 0
    nchunk = n_pages // npg
    nblk = n_pages * page // MOBA_BLOCK
    nrow = N_HEADS * dec_seq
    qd = q.reshape(batch, dec_seq, N_HEADS, HEAD_DIM).transpose(0, 2, 1, 3)
    qd = qd.reshape(batch, nrow, HEAD_DIM)
    per_b = lambda b, pt: (b, 0, 0)
    new_rows = dec_seq * N_KV_HEADS
    grid_spec = pltpu.PrefetchScalarGridSpec(
        num_scalar_prefetch=1,
        grid=(batch,),
        in_specs=[
            pl.BlockSpec((None, nrow, HEAD_DIM), per_b),
            pl.BlockSpec((None, new_rows, HEAD_DIM), per_b),
            pl.BlockSpec((None, new_rows, HEAD_DIM), per_b),
            pl.BlockSpec(memory_space=pl.ANY),
            pl.BlockSpec(memory_space=pl.ANY),
        ],
        out_specs=pl.BlockSpec((None, nrow, HEAD_DIM), per_b),
        scratch_shapes=[
            pltpu.VMEM((n_pages, prow, HEAD_DIM), F32),
            pltpu.VMEM((n_pages, prow, HEAD_DIM), F32),
            pltpu.VMEM((nrow, n_pages * prow), F32),
            pltpu.VMEM((2 * nrow, n_pages * prow), BF16),
            pltpu.VMEM((nblk * SUBLANES, HEAD_DIM), F32),
            pltpu.SemaphoreType.DMA((nchunk,)),
            pltpu.SemaphoreType.DMA((nchunk,)),
        ],
    )
    o = pl.pallas_call(
        functools.partial(_attn_sample_kernel, base=layer * n_pool, npg=npg, dec_seq=dec_seq),
        grid_spec=grid_spec,
        out_shape=jax.ShapeDtypeStruct((batch, nrow, HEAD_DIM), F32),
        compiler_params=_cparams(("arbitrary",)),
        name="attn_sample",
    )(page_table, qd, k_new.reshape(batch, new_rows, HEAD_DIM),
      v_new.reshape(batch, new_rows, HEAD_DIM), cache_k, cache_v)
    o = o.reshape(batch, N_HEADS, dec_seq, HEAD_DIM).transpose(0, 2, 1, 3)
    return o.reshape(batch * dec_seq, N_HEADS * HEAD_DIM)


def _oproj_kernel(o_ref, h_ref, wo_ref, out_ref):
    out_ref[...] = h_ref[...] + _dot3(o_ref[...], wo_ref[...])


def _oproj(o, h, wo, layer):
    n, d = h.shape
    full = pl.BlockSpec((n, d), lambda i: (0, 0))
    return pl.pallas_call(
        _oproj_kernel,
        grid=(1,),
        in_specs=[pl.BlockSpec(o.shape, lambda i: (0, 0)), full,
                  _layer_spec(wo.shape[1:], layer)],
        out_specs=full,
        out_shape=jax.ShapeDtypeStruct(h.shape, F32),
        compiler_params=_cparams(("arbitrary",)),
        name="oproj_sample",
    )(o, h, wo)


def _route(logits, lane):
    first_expert_lane = N_EXPERT_GROUPS
    is_group = lane < N_EXPERT_GROUPS
    gl = jnp.where(is_group, logits, NEG_INF)
    ge = jnp.where(is_group, jnp.exp(gl - jnp.max(gl, axis=-1, keepdims=True)), 0.0)
    gprob = ge / jnp.sum(ge, axis=-1, keepdims=True)
    g_w = jnp.max(gprob, axis=-1, keepdims=True)
    g_idx = jnp.min(jnp.where(is_group & (gprob == g_w), lane, LANES), axis=-1, keepdims=True)
    lo = first_expert_lane + g_idx * EXPERTS_PER_GROUP
    in_group = (lane >= lo) & (lane < lo + EXPERTS_PER_GROUP)
    el = jnp.where(in_group, logits, NEG_INF)
    ee = jnp.where(in_group, jnp.exp(el - jnp.max(el, axis=-1, keepdims=True)), 0.0)
    eprob = jnp.where(in_group, ee / jnp.sum(ee, axis=-1, keepdims=True), -1.0)
    p1 = jnp.max(eprob, axis=-1, keepdims=True)
    i1 = jnp.min(jnp.where(eprob == p1, lane, LANES), axis=-1, keepdims=True)
    rest = jnp.where(lane == i1, -1.0, eprob)
    p2 = jnp.max(rest, axis=-1, keepdims=True)
    i2 = jnp.min(jnp.where(rest == p2, lane, LANES), axis=-1, keepdims=True)
    denom = p1 + p2
    w1 = p1 / denom * g_w
    w2 = p2 / denom * g_w
    return g_idx, jnp.where(lane == i1, w1, 0.0) + jnp.where(lane == i2, w2, 0.0)


def _moe_kernel(h_ref, n2_ref, wr_ref, br_ref, wg_ref, wu_ref, wd_ref, out_ref,
                xn_scr, hid_scr, *, precise):
    g = pl.program_id(1)
    tm = h_ref.shape[0]
    de = wg_ref.shape[2]
    lane = lax.broadcasted_iota(jnp.int32, (tm, LANES), 1)
    first_expert_lane = N_EXPERT_GROUPS

    @pl.when(g == 0)
    def _():
        x = h_ref[...]
        xn = _rms(x, n2_ref[...])
        xn_scr[...] = xn.astype(xn_scr.dtype)
        out_ref[...] = x

    xb = xn_scr[...]
    _, gates = _route(_mm(xb, wr_ref[...], precise) + br_ref[...], lane)
    for e in range(EXPERTS_PER_GROUP):
        a = _mm(xb, wg_ref[e], precise)
        b = _mm(xb, wu_ref[e], precise)
        tgt = first_expert_lane + g * EXPERTS_PER_GROUP + e
        gate = jnp.sum(jnp.where(lane == tgt, gates, 0.0), axis=-1, keepdims=True)
        hid_scr[:, e * de:(e + 1) * de] = (_silu(a) * b * gate).astype(hid_scr.dtype)
    out_ref[...] += _mm(hid_scr[...], wd_ref[...], precise)


def _moe(h, n2, wr, br, wg, wu, wd, layer, *, tm, precise):
    n, d = h.shape
    de = wg.shape[3]
    epg = EXPERTS_PER_GROUP
    const = lambda i, g: (0, 0)
    group = lambda i, g: (layer, g, 0, 0)
    wd = wd.reshape(wd.shape[0], N_EXPERT_GROUPS, epg * de, d)
    work = F32 if precise else BF16
    return pl.pallas_call(
        functools.partial(_moe_kernel, precise=precise),
        grid=(n // tm, N_EXPERT_GROUPS),
        in_specs=[
            pl.BlockSpec((tm, d), lambda i, g: (i, 0)),
            pl.BlockSpec((1, d), const),
            pl.BlockSpec((d, LANES), const),
            pl.BlockSpec((1, LANES), const),
            pl.BlockSpec((None, epg, d, de), group),
            pl.BlockSpec((None, epg, d, de), group),
            pl.BlockSpec((None, None, epg * de, d), group),
        ],
        out_specs=pl.BlockSpec((tm, d), lambda i, g: (i, 0)),
        out_shape=jax.ShapeDtypeStruct((n, d), F32),
        scratch_shapes=[
            pltpu.VMEM((tm, d), work),
            pltpu.VMEM((tm, epg * de), work),
        ],
        compiler_params=_cparams(("parallel", "arbitrary")),
        name="moe_sample" if precise else "moe",
    )(h, n2, wr, br, wg, wu, wd)


def kernel(x_prompt, x_sample, cache_k, cache_v, page_table, norm1, norm2, a_w_in, a_g_v, a_w_s, a_b_s, a_w_out, b_w_qkv, b_q_norm, b_k_norm, b_w_o, moe_w_group, moe_b_group, moe_w_router, moe_b_router, moe_w_gate, moe_w_up, moe_w_down):
    bp, tp, d = x_prompt.shape
    bs, ts, _ = x_sample.shape
    depth = norm1.shape[0]
    n_attn, n_pool, page = cache_k.shape[0], cache_k.shape[1], cache_k.shape[2]
    past_len = page_table.shape[1] * page
    assert CHUNK % ts == 0 and (bs * ts) % CHUNK == 0 and tp % MOBA_BLOCK == 0

    hp = x_prompt.reshape(bp * tp, d)
    hs = x_sample.reshape(bs * ts, d)
    ck = cache_k.reshape(n_attn * n_pool, page * N_KV_HEADS, HEAD_DIM)
    cv = cache_v.reshape(n_attn * n_pool, page * N_KV_HEADS, HEAD_DIM)
    tab_p = _rope_tables(jnp.arange(tp, dtype=jnp.int32))
    pos_s = past_len + jnp.arange(ts, dtype=jnp.int32)
    tab_s = tuple(jnp.tile(t, (bs, 1)) for t in _rope_tables(pos_s))

    pad = LANES - N_EXPERT_GROUPS - N_EXPERTS
    wr_all = jnp.concatenate(
        [moe_w_group, moe_w_router, jnp.zeros((depth, d, pad), F32)], axis=2)
    br_all = jnp.concatenate(
        [moe_b_group, moe_b_router, jnp.zeros((depth, pad), F32)], axis=1)[:, None, :]

    w_in_bf, w_out_bf = a_w_in.astype(BF16), a_w_out.astype(BF16)
    w_qkv_bf, w_o_bf = b_w_qkv.astype(BF16), b_w_o.astype(BF16)
    wg_bf, wu_bf, wd_bf = moe_w_gate.astype(BF16), moe_w_up.astype(BF16), moe_w_down.astype(BF16)

    new_k_p, new_v_p, new_k_s, new_v_s, new_chunk_v = [], [], [], [], []
    reps = CHUNK // ts
    for i in range(depth):
        j = i // 2
        n1 = norm1[i][None, :]
        if i % 2 == 0:
            g_v = a_g_v[j][None, :]
            hp = _mixer_a(hp, n1, w_in_bf, g_v, a_w_s[j], a_b_s[j][:, :, None], w_out_bf, j,
                          tm=1024, seq_len=CHUNK, emit_v=False, precise=False, nsplit=2)
            wmix_s = jnp.tile(a_w_s[j][:, :ts, :ts], (1, reps, reps))
            bs_s = jnp.tile(a_b_s[j][:, :ts], (1, reps))[:, :, None]
            hs, v_rows = _mixer_a(hs, n1, a_w_in, g_v, wmix_s, bs_s, a_w_out, j,
                                  tm=CHUNK, seq_len=ts, emit_v=True, precise=True)
            new_chunk_v.append(v_rows.reshape(bs, ts, -1))
        else:
            qn = b_q_norm[j][None, :]
            kn = b_k_norm[j][None, :]
            qp, kp, vp = _qkv(hp, n1, w_qkv_bf, j, qn, kn, tab_p, tm=1024, precise=False, nsplit=2)
            qs, ks_new, vs_new = _qkv(hs, n1, b_w_qkv, j, qn, kn, tab_s, tm=bs * ts, precise=True)
            hp = _attn_prompt(qp, kp, vp, hp, w_o_bf, j, batch=bp, seq=tp)
            o_s = _attn_sample(qs, ks_new, vs_new, ck, cv, page_table, j, n_pool=n_pool,
                               batch=bs, dec_seq=ts)
            hs = _oproj(o_s, hs, b_w_o, j)
            new_k_p.append(kp.reshape(bp, tp, N_KV_HEADS, HEAD_DIM))
            new_v_p.append(vp.reshape(bp, tp, N_KV_HEADS, HEAD_DIM))
            new_k_s.append(ks_new.reshape(bs, ts, N_KV_HEADS, HEAD_DIM))
            new_v_s.append(vs_new.reshape(bs, ts, N_KV_HEADS, HEAD_DIM))
        n2 = norm2[i][None, :]
        hp = _moe(hp, n2, wr_all[i], br_all[i], wg_bf, wu_bf, wd_bf, i,
                  tm=min(1024, bp * tp), precise=False)
        hs = _moe(hs, n2, wr_all[i], br_all[i], moe_w_gate, moe_w_up, moe_w_down, i,
                  tm=bs * ts, precise=True)
    return (hp.reshape(bp, tp, d), hs.reshape(bs, ts, d),
            jnp.stack(new_k_p), jnp.stack(new_v_p), jnp.stack(new_k_s), jnp.stack(new_v_s),
            jnp.stack(new_chunk_v))
```
